```python
import math
import jax, jax.numpy as jnp
from jax import lax
import numpy as np

D_MODEL = 2048
BATCH = 1
SEQ = 8192
DEPTH = 4

GRID_W = 64
MIX_WIDTH = D_MODEL
DIFF_HEADS = 8
DIFF_QK_DIM = MIX_WIDTH // (4 * DIFF_HEADS)
DIFF_V_DIM = 2 * DIFF_QK_DIM
DIFF_W = DIFF_HEADS * DIFF_V_DIM
NA_HEADS = 8
NA_HEAD_DIM = MIX_WIDTH // (2 * NA_HEADS)
NA_W = NA_HEADS * NA_HEAD_DIM
NA_WIN_ROWS = 8
NA_WIN_COLS = 16
IN_COLS = 3 * DIFF_W + 3 * NA_W
D_FF = 4 * D_MODEL
ROPE_THETA = 10000.0
EPS = 1e-6
Q_BLOCK = 128

kernel_name = "hybrid_diffattn_natten_encoder"


def rmsnorm(x, g):
    xf = x.astype(jnp.float32)
    y = xf * lax.rsqrt(jnp.mean(xf * xf, axis=-1, keepdims=True) + EPS)
    return (y * g.astype(jnp.float32)).astype(x.dtype)


def rope_tables(seq, dim):
    inv_freq = 1.0 / (ROPE_THETA ** (jnp.arange(0, dim, 2, dtype=jnp.float32) / dim))
    ang = jnp.arange(seq, dtype=jnp.float32)[:, None] * inv_freq[None, :]
    return jnp.cos(ang), jnp.sin(ang)


def apply_rope(x, cos, sin):
    xf = x.astype(jnp.float32)
    x1, x2 = jnp.split(xf, 2, axis=-1)
    c = cos[None, :, None, None, :]
    s = sin[None, :, None, None, :]
    return jnp.concatenate([x1 * c - x2 * s, x2 * c + x1 * s], axis=-1).astype(x.dtype)


def diff_attention(q, k, v, lam, lambda_init, subln_g):
    b, s, h, _, d = q.shape
    nblk = s // Q_BLOCK
    scale = d ** -0.5
    qb = jnp.moveaxis(q.reshape(b, nblk, Q_BLOCK, h, 2, d), 1, 0)

    def block(qi):
        sc = jnp.einsum('bqhmd,bkhmd->bhmqk', qi, k).astype(jnp.float32) * scale
        p = jax.nn.softmax(sc, axis=-1)
        w = (p[:, :, 0] - lam * p[:, :, 1]).astype(v.dtype)
        return jnp.einsum('bhqk,bkhe->bqhe', w, v)

    o = lax.map(block, qb)
    o = jnp.moveaxis(o, 0, 1).reshape(b, s, h, DIFF_V_DIM)
    o = rmsnorm(o, subln_g) * (1.0 - lambda_init)
    return o.reshape(b, s, h * DIFF_V_DIM)


def neighbourhood_attention(q, k, v, rpb):
    b, s, h, d = q.shape
    rows = s // GRID_W
    kr = min(NA_WIN_ROWS, rows)
    kc = NA_WIN_COLS
    scale = d ** -0.5
    qg = q.reshape(b, rows, GRID_W, h, d)
    kg = k.reshape(b, rows, GRID_W, h, d)
    vg = v.reshape(b, rows, GRID_W, h, d)
    c = np.arange(GRID_W)
    cs = np.clip(c - kc // 2, 0, GRID_W - kc)
    col_idx = cs[:, None] + np.arange(kc)
    col_rel = col_idx - c[:, None] + (NA_WIN_COLS - 1)

    def row_block(r):
        rs = jnp.clip(r - kr // 2, 0, rows - kr)
        q_r = lax.dynamic_index_in_dim(qg, r, axis=1, keepdims=False)
        k_rows = lax.dynamic_slice_in_dim(kg, rs, kr, axis=1)
        v_rows = lax.dynamic_slice_in_dim(vg, rs, kr, axis=1)
        k_nb = k_rows[:, :, col_idx]
        v_nb = v_rows[:, :, col_idx]
        row_rel = rs + jnp.arange(kr) - r + (NA_WIN_ROWS - 1)
        bias = jnp.take(rpb, row_rel, axis=1)[:, :, col_rel]
        bias = jnp.transpose(bias, (0, 2, 1, 3)).astype(jnp.float32)
        sc = jnp.einsum('bchd,brcjhd->bhcrj', q_r, k_nb).astype(jnp.float32) * scale + bias[None]
        p = jax.nn.softmax(sc.reshape(b, h, GRID_W, kr * kc), axis=-1).reshape(b, h, GRID_W, kr, kc)
        return jnp.einsum('bhcrj,brcjhd->bchd', p.astype(v.dtype), v_nb)

    o = lax.map(row_block, jnp.arange(rows))
    return jnp.moveaxis(o, 0, 1).reshape(b, s, h * d)


def setup_inputs(seed: int = 0) -> dict:
    key = jax.random.key(seed)
    ks = jax.random.split(key, 15)
    f32 = jnp.float32
    nrm = lambda k, shape, sc: jax.random.normal(k, shape, f32) * sc
    return {
        "x": nrm(ks[0], (BATCH, SEQ, D_MODEL), 1.0),
        "attn_norm": 1.0 + nrm(ks[1], (DEPTH, D_MODEL), 0.05),
        "w_in": nrm(ks[2], (DEPTH, D_MODEL, IN_COLS), D_MODEL ** -0.5),
        "lambda_q1": nrm(ks[3], (DEPTH, DIFF_QK_DIM), 0.1),
        "lambda_k1": nrm(ks[4], (DEPTH, DIFF_QK_DIM), 0.1),
        "lambda_q2": nrm(ks[5], (DEPTH, DIFF_QK_DIM), 0.1),
        "lambda_k2": nrm(ks[6], (DEPTH, DIFF_QK_DIM), 0.1),
        "diff_subln": 1.0 + nrm(ks[7], (DEPTH, DIFF_V_DIM), 0.05),
        "na_norm": 1.0 + nrm(ks[8], (DEPTH, NA_W), 0.05),
        "na_rpb": nrm(ks[9], (DEPTH, NA_HEADS, 2 * NA_WIN_ROWS - 1, 2 * NA_WIN_COLS - 1), 0.1),
        "w_out": nrm(ks[10], (DEPTH, MIX_WIDTH, D_MODEL), MIX_WIDTH ** -0.5),
        "mlp_norm": 1.0 + nrm(ks[11], (DEPTH, D_MODEL), 0.05),
        "w_mlp_in": nrm(ks[12], (DEPTH, D_MODEL, D_FF), D_MODEL ** -0.5),
        "w_mlp_out": nrm(ks[13], (DEPTH, D_FF, D_MODEL), D_FF ** -0.5),
        "final_norm": 1.0 + nrm(ks[14], (D_MODEL,), 0.05),
    }


def reference(x, attn_norm, w_in, lambda_q1, lambda_k1, lambda_q2, lambda_k2, diff_subln,
              na_norm, na_rpb, w_out, mlp_norm, w_mlp_in, w_mlp_out, final_norm):
    b, s, _ = x.shape
    cos, sin = rope_tables(s, DIFF_QK_DIM)
    split_pts = [DIFF_W, 2 * DIFF_W, 3 * DIFF_W, 3 * DIFF_W + NA_W, 3 * DIFF_W + 2 * NA_W]
    for l in range(DEPTH):
        lambda_init = 0.8 - 0.6 * math.exp(-0.3 * l)
        h = rmsnorm(x, attn_norm[l])
        proj = jnp.einsum('bsd,dc->bsc', h, w_in[l])
        qd, kd, vd, qn, kn, vn = jnp.split(proj, split_pts, axis=-1)
        qd = apply_rope(qd.reshape(b, s, DIFF_HEADS, 2, DIFF_QK_DIM), cos, sin)
        kd = apply_rope(kd.reshape(b, s, DIFF_HEADS, 2, DIFF_QK_DIM), cos, sin)
        vd = vd.reshape(b, s, DIFF_HEADS, DIFF_V_DIM)
        lam = (jnp.exp(jnp.sum(lambda_q1[l].astype(jnp.float32) * lambda_k1[l].astype(jnp.float32)))
               - jnp.exp(jnp.sum(lambda_q2[l].astype(jnp.float32) * lambda_k2[l].astype(jnp.float32)))
               + lambda_init)
        o_diff = diff_attention(qd, kd, vd, lam, lambda_init, diff_subln[l])
        o_na = neighbourhood_attention(qn.reshape(b, s, NA_HEADS, NA_HEAD_DIM),
                                       kn.reshape(b, s, NA_HEADS, NA_HEAD_DIM),
                                       vn.reshape(b, s, NA_HEADS, NA_HEAD_DIM),
                                       na_rpb[l])
        o_na = rmsnorm(o_na, na_norm[l])
        mix = jnp.concatenate([o_diff, o_na], axis=-1)
        x = x + jnp.einsum('bsc,cd->bsd', mix, w_out[l])
        h = rmsnorm(x, mlp_norm[l])
        u = jax.nn.relu(jnp.einsum('bsd,df->bsf', h, w_mlp_in[l]))
        x = x + jnp.einsum('bsf,fd->bsd', u * u, w_mlp_out[l])
    return rmsnorm(x, final_norm)
```

```python
import functools
import math

import jax
import jax.numpy as jnp
import numpy as np
from jax import lax
from jax.experimental import pallas as pl
from jax.experimental.pallas import tpu as pltpu

F32 = jnp.float32
BF16 = jnp.bfloat16

GRID_W = 64
N_DIFF_HEADS = 8
DIFF_QK_DIM = 64
DIFF_V_DIM = 128
DIFF_W = N_DIFF_HEADS * DIFF_V_DIM
N_NA_HEADS = 8
NA_HEAD_DIM = 128
NA_W = N_NA_HEADS * NA_HEAD_DIM
NA_WIN_ROWS = 8
NA_WIN_COLS = 16
ROPE_THETA = 10000.0
EPS = 1e-6
MASK_VALUE = -1e30

LANES = 128
VMEM_LIMIT_BYTES = 56 * 1024 * 1024

MM_TM = 1024
MM_TN = 1024
MLP_OUT_TK = 512
ATT_TQ = 256
ATT_TK = 512


def _compiler_params(semantics):
    return pltpu.CompilerParams(dimension_semantics=semantics,
                                vmem_limit_bytes=VMEM_LIMIT_BYTES)


def _rmsnorm_rows(x, g):
    return x * lax.rsqrt(jnp.mean(x * x, axis=-1, keepdims=True) + EPS) * g


def _norm_matmul_body(x_ref, g_ref, w_ref, o_ref, h_ref, acc_epilogue):
    @pl.when(pl.program_id(1) == 0)
    def _():
        h_ref[...] = _rmsnorm_rows(x_ref[...], g_ref[...]).astype(BF16)

    acc = jnp.dot(h_ref[...], w_ref[...], preferred_element_type=F32)
    acc_epilogue(acc, o_ref)


def _in_proj_kernel(x_ref, g_ref, w_ref, cos_ref, sin_lo_ref, sin_hi_ref, o_ref, h_ref):
    j = pl.program_id(1)
    n_rope_blocks = 2 * DIFF_W // MM_TN
    n_q_blocks = DIFF_W // MM_TN

    def epilogue(acc, o_ref):
        @pl.when(j < n_rope_blocks)
        def _():
            scale = jnp.where(j < n_q_blocks, DIFF_QK_DIM ** -0.5, 1.0).astype(F32)
            cos = cos_ref[...]
            sin_lo = sin_lo_ref[...]
            sin_hi = sin_hi_ref[...]
            for c in range(MM_TN // LANES):
                xs = acc[:, c * LANES:(c + 1) * LANES]
                r = (xs * cos
                     + pltpu.roll(xs, LANES - DIFF_QK_DIM // 2, 1) * sin_lo
                     + pltpu.roll(xs, DIFF_QK_DIM // 2, 1) * sin_hi)
                o_ref[:, c * LANES:(c + 1) * LANES] = (r * scale).astype(o_ref.dtype)

        @pl.when(j >= n_rope_blocks)
        def _():
            o_ref[...] = acc.astype(o_ref.dtype)

    _norm_matmul_body(x_ref, g_ref, w_ref, o_ref, h_ref, epilogue)


def _mlp_in_kernel(x_ref, g_ref, w_ref, o_ref, h_ref):
    def epilogue(acc, o_ref):
        u = jnp.maximum(acc, 0.0)
        o_ref[...] = (u * u).astype(o_ref.dtype)

    _norm_matmul_body(x_ref, g_ref, w_ref, o_ref, h_ref, epilogue)


def _norm_matmul(kernel_fn, x, g, w, extra=(), name=None):
    s, d = x.shape
    n = w.shape[1]
    tm = min(MM_TM, s)
    grid = (s // tm, n // MM_TN)
    extra_specs = [pl.BlockSpec((tm, LANES), lambda i, j: (i, 0)) for _ in extra]
    return pl.pallas_call(
        kernel_fn,
        grid=grid,
        in_specs=[pl.BlockSpec((tm, d), lambda i, j: (i, 0)),
                  pl.BlockSpec((1, d), lambda i, j: (0, 0)),
                  pl.BlockSpec((d, MM_TN), lambda i, j: (0, j))] + extra_specs,
        out_specs=pl.BlockSpec((tm, MM_TN), lambda i, j: (i, j)),
        out_shape=jax.ShapeDtypeStruct((s, n), BF16),
        scratch_shapes=[pltpu.VMEM((tm, d), BF16)],
        compiler_params=_compiler_params(("arbitrary", "arbitrary")),
        name=name,
    )(x, g.reshape(1, d), w, *extra)


def _diff_attn_kernel(lam_ref, q_ref, k_ref, v_ref, lq1_ref, lk1_ref, lq2_ref, lk2_ref,
                      g_ref, o_ref, vt_ref, acc_ref, *, tq, tk, n_chunks):
    @pl.when(pl.program_id(1) == 0)
    def _():
        for c in range(n_chunks):
            vt_ref[c] = v_ref[c * tk:(c + 1) * tk, :].astype(F32).T.astype(BF16)

    qt = q_ref[...].astype(F32).T
    row = lax.broadcasted_iota(jnp.int32, qt.shape, 0)
    qpad = jnp.concatenate([jnp.where(row < DIFF_QK_DIM, qt, 0.0),
                            jnp.where(row >= DIFF_QK_DIM, qt, 0.0)], axis=1).astype(BF16)

    acc_ref[...] = jnp.zeros_like(acc_ref)

    def chunk(c, carry):
        m, l = carry
        k = k_ref[pl.ds(pl.multiple_of(c * tk, tk), tk), :]
        s = jnp.dot(k, qpad, preferred_element_type=F32)
        m_new = jnp.maximum(m, jnp.max(s, axis=0, keepdims=True))
        alpha = jnp.exp(m - m_new)
        p = jnp.exp(s - m_new)
        l = alpha * l + jnp.sum(p, axis=0, keepdims=True)
        pv = jnp.dot(vt_ref[c], p.astype(BF16), preferred_element_type=F32)
        acc_ref[...] = acc_ref[...] * alpha + pv
        return m_new, l

    m0 = jnp.full((1, 2 * tq), MASK_VALUE, F32)
    l0 = jnp.zeros((1, 2 * tq), F32)
    _, l = lax.fori_loop(0, n_chunks, chunk, (m0, l0))

    lambda_init = lam_ref[0]
    lam = (jnp.exp(jnp.sum(lq1_ref[...] * lk1_ref[...], axis=-1, keepdims=True))
           - jnp.exp(jnp.sum(lq2_ref[...] * lk2_ref[...], axis=-1, keepdims=True))
           + lambda_init)
    o = acc_ref[...] / l
    od = (o[:, :tq] - lam * o[:, tq:]).T
    y = _rmsnorm_rows(od, g_ref[...]) * (1.0 - lambda_init)
    o_ref[...] = y.astype(o_ref.dtype)


def _diff_attention(proj, lam_init, lq1, lk1, lq2, lk2, subln_g):
    s = proj.shape[0]
    tq = min(ATT_TQ, s)
    tk = min(ATT_TK, s)
    n_chunks = s // tk
    kernel_fn = functools.partial(_diff_attn_kernel, tq=tq, tk=tk, n_chunks=n_chunks)
    k_col0 = DIFF_W // LANES
    v_col0 = 2 * DIFF_W // LANES
    vec = lambda n: pl.BlockSpec((1, n), lambda h, i: (0, 0))
    return pl.pallas_call(
        kernel_fn,
        grid=(N_DIFF_HEADS, s // tq),
        in_specs=[pl.BlockSpec(memory_space=pltpu.SMEM),
                  pl.BlockSpec((tq, LANES), lambda h, i: (i, h)),
                  pl.BlockSpec((s, LANES), lambda h, i: (0, k_col0 + h)),
                  pl.BlockSpec((s, LANES), lambda h, i: (0, v_col0 + h)),
                  vec(DIFF_QK_DIM), vec(DIFF_QK_DIM), vec(DIFF_QK_DIM), vec(DIFF_QK_DIM),
                  vec(DIFF_V_DIM)],
        out_specs=pl.BlockSpec((tq, LANES), lambda h, i: (i, h)),
        out_shape=jax.ShapeDtypeStruct((s, DIFF_W), BF16),
        scratch_shapes=[pltpu.VMEM((n_chunks, DIFF_V_DIM, tk), BF16),
                        pltpu.VMEM((DIFF_V_DIM, 2 * tq), F32)],
        compiler_params=_compiler_params(("arbitrary", "arbitrary")),
        name="diff_attention",
    )(lam_init, proj, proj, proj,
      lq1.reshape(1, -1), lk1.reshape(1, -1), lq2.reshape(1, -1), lk2.reshape(1, -1),
      subln_g.reshape(1, -1))


def _window_start(r, rows, kr):
    return jnp.clip(r - kr // 2, 0, rows - kr)


def _na_attn_kernel(*refs, kr):
    q_ref = refs[0]
    k_refs = refs[1:1 + kr]
    v_refs = refs[1 + kr:1 + 2 * kr]
    bias_ref, g_ref, o_ref, kwin_ref, vwin_ref, o_scr = refs[1 + 2 * kr:]
    for i in range(kr):
        kwin_ref[i * GRID_W:(i + 1) * GRID_W, :] = k_refs[i][...]
        vwin_ref[i * GRID_W:(i + 1) * GRID_W, :] = v_refs[i][...]
    scale = NA_HEAD_DIM ** -0.5
    for h in range(N_NA_HEADS):
        cols = slice(h * NA_HEAD_DIM, (h + 1) * NA_HEAD_DIM)
        sc = lax.dot_general(q_ref[:, cols], kwin_ref[:, cols], (((1,), (1,)), ((), ())),
                             preferred_element_type=F32)
        sc = sc * scale + bias_ref[h]
        p = jnp.exp(sc - jnp.max(sc, axis=-1, keepdims=True))
        l = jnp.sum(p, axis=-1, keepdims=True)
        pv = jnp.dot(p.astype(BF16), vwin_ref[:, cols], preferred_element_type=F32)
        o_scr[:, cols] = pv / l
    o_ref[...] = _rmsnorm_rows(o_scr[...], g_ref[...]).astype(o_ref.dtype)


def _na_bias_table(rpb, kr):
    c = np.arange(GRID_W)
    cs = np.clip(c - NA_WIN_COLS // 2, 0, GRID_W - NA_WIN_COLS)
    kc = np.arange(GRID_W)
    valid = (kc[None, :] >= cs[:, None]) & (kc[None, :] < cs[:, None] + NA_WIN_COLS)
    col_rel = np.clip(kc[None, :] - c[:, None] + NA_WIN_COLS - 1, 0, 2 * NA_WIN_COLS - 2)
    d = np.arange(kr)
    row_rel = np.arange(kr)[None, :] - d[:, None] + NA_WIN_ROWS - 1
    tbl = rpb.astype(F32)[:, row_rel][:, :, :, col_rel]
    tbl = jnp.where(valid[None, None, None], tbl, MASK_VALUE)
    tbl = jnp.transpose(tbl, (1, 0, 3, 2, 4))
    return tbl.reshape(kr, N_NA_HEADS, GRID_W, kr * GRID_W)


def _na_attention(proj, rpb, na_g):
    s = proj.shape[0]
    rows = s // GRID_W
    kr = min(NA_WIN_ROWS, rows)
    bias = _na_bias_table(rpb, kr)
    q_col, k_col, v_col = 3 * DIFF_W // NA_W, 3 * DIFF_W // NA_W + 1, 3 * DIFF_W // NA_W + 2

    def kv_spec(i, col):
        return pl.BlockSpec((GRID_W, NA_W), lambda r: (_window_start(r, rows, kr) + i, col))

    return pl.pallas_call(
        functools.partial(_na_attn_kernel, kr=kr),
        grid=(rows,),
        in_specs=[pl.BlockSpec((GRID_W, NA_W), lambda r: (r, q_col))]
                 + [kv_spec(i, k_col) for i in range(kr)]
                 + [kv_spec(i, v_col) for i in range(kr)]
                 + [pl.BlockSpec((None, N_NA_HEADS, GRID_W, kr * GRID_W),
                                 lambda r: (r - _window_start(r, rows, kr), 0, 0, 0)),
                    pl.BlockSpec((1, NA_W), lambda r: (0, 0))],
        out_specs=pl.BlockSpec((GRID_W, NA_W), lambda r: (r, 0)),
        out_shape=jax.ShapeDtypeStruct((s, NA_W), BF16),
        scratch_shapes=[pltpu.VMEM((kr * GRID_W, NA_W), BF16),
                        pltpu.VMEM((kr * GRID_W, NA_W), BF16),
                        pltpu.VMEM((GRID_W, NA_W), F32)],
        compiler_params=_compiler_params(("arbitrary",)),
        name="na_attention",
    )(*([proj] * (1 + 2 * kr)), bias, na_g.reshape(1, -1))


def _out_proj_kernel(a_ref, b_ref, w_ref, x_ref, o_ref):
    ka = a_ref.shape[1]
    acc = jnp.dot(a_ref[...], w_ref[:ka, :], preferred_element_type=F32)
    acc += jnp.dot(b_ref[...], w_ref[ka:, :], preferred_element_type=F32)
    o_ref[...] = x_ref[...] + acc


def _out_proj(o_diff, o_na, w, x):
    s, d = x.shape
    tm = min(MM_TM, s)
    ka, kb = o_diff.shape[1], o_na.shape[1]
    return pl.pallas_call(
        _out_proj_kernel,
        grid=(s // tm, d // MM_TN),
        in_specs=[pl.BlockSpec((tm, ka), lambda i, j: (i, 0)),
                  pl.BlockSpec((tm, kb), lambda i, j: (i, 0)),
                  pl.BlockSpec((ka + kb, MM_TN), lambda i, j: (0, j)),
                  pl.BlockSpec((tm, MM_TN), lambda i, j: (i, j))],
        out_specs=pl.BlockSpec((tm, MM_TN), lambda i, j: (i, j)),
        out_shape=jax.ShapeDtypeStruct((s, d), F32),
        compiler_params=_compiler_params(("arbitrary", "arbitrary")),
        name="out_proj",
    )(o_diff, o_na, w, x)


def _mlp_out_kernel(u_ref, w_ref, x_ref, *rest, final_norm):
    o_ref = rest[-1]
    k = pl.program_id(1)

    @pl.when(k == 0)
    def _():
        o_ref[...] = x_ref[...]

    o_ref[...] += jnp.dot(u_ref[...], w_ref[...], preferred_element_type=F32)

    if final_norm:
        @pl.when(k == pl.num_programs(1) - 1)
        def _():
            o_ref[...] = _rmsnorm_rows(o_ref[...], rest[0][...])


def _mlp_out(u, w, x, final_g=None):
    s, d = x.shape
    f = u.shape[1]
    tm = min(MM_TM, s)
    extra, extra_specs = (), []
    if final_g is not None:
        extra = (final_g.reshape(1, d),)
        extra_specs = [pl.BlockSpec((1, d), lambda i, k: (0, 0))]
    return pl.pallas_call(
        functools.partial(_mlp_out_kernel, final_norm=final_g is not None),
        grid=(s // tm, f // MLP_OUT_TK),
        in_specs=[pl.BlockSpec((tm, MLP_OUT_TK), lambda i, k: (i, k)),
                  pl.BlockSpec((MLP_OUT_TK, d), lambda i, k: (k, 0)),
                  pl.BlockSpec((tm, d), lambda i, k: (i, 0))] + extra_specs,
        out_specs=pl.BlockSpec((tm, d), lambda i, k: (i, 0)),
        out_shape=jax.ShapeDtypeStruct((s, d), F32),
        compiler_params=_compiler_params(("arbitrary", "arbitrary")),
        name="mlp_out_final" if final_g is not None else "mlp_out",
    )(u, w, x, *extra)


def _rope_tables(seq):
    inv_freq = 1.0 / (ROPE_THETA ** (jnp.arange(0, DIFF_QK_DIM, 2, dtype=F32) / DIFF_QK_DIM))
    ang = jnp.arange(seq, dtype=F32)[:, None] * inv_freq[None, :]
    cos, sin = jnp.cos(ang), jnp.sin(ang)
    zero = jnp.zeros_like(sin)
    reps = LANES // DIFF_QK_DIM
    cos_t = jnp.tile(cos, (1, 2 * reps))
    sin_lo = jnp.tile(jnp.concatenate([-sin, zero], axis=1), (1, reps))
    sin_hi = jnp.tile(jnp.concatenate([zero, sin], axis=1), (1, reps))
    return cos_t, sin_lo, sin_hi


def kernel(x, attn_norm, w_in, lambda_q1, lambda_k1, lambda_q2, lambda_k2, diff_subln, na_norm,
           na_rpb, w_out, mlp_norm, w_mlp_in, w_mlp_out, final_norm):
    b, s, d = x.shape
    depth = w_in.shape[0]
    rope = _rope_tables(s)
    outs = []
    for bi in range(b):
        xb = x[bi].astype(F32)
        for l in range(depth):
            lambda_init = 0.8 - 0.6 * math.exp(-0.3 * l)
            proj = _norm_matmul(_in_proj_kernel, xb, attn_norm[l].astype(F32),
                                w_in[l].astype(BF16), rope, name="in_proj")
            o_diff = _diff_attention(proj, jnp.full((1,), lambda_init, F32),
                                     lambda_q1[l].astype(F32), lambda_k1[l].astype(F32),
                                     lambda_q2[l].astype(F32), lambda_k2[l].astype(F32),
                                     diff_subln[l].astype(F32))
            o_na = _na_attention(proj, na_rpb[l], na_norm[l].astype(F32))
            xb = _out_proj(o_diff, o_na, w_out[l].astype(BF16), xb)
            u = _norm_matmul(_mlp_in_kernel, xb, mlp_norm[l].astype(F32),
                             w_mlp_in[l].astype(BF16), name="mlp_in")
            xb = _mlp_out(u, w_mlp_out[l].astype(BF16), xb,
                          final_norm.astype(F32) if l == depth - 1 else None)
        outs.append(xb)
    out = outs[0][None] if b == 1 else jnp.stack(outs)
    return out.astype(x.dtype)
```

```python
import functools
import math

import jax
import jax.numpy as jnp
import numpy as np
from jax import lax
from jax.experimental import pallas as pl
from jax.experimental.pallas import tpu as pltpu

F32 = jnp.float32
BF16 = jnp.bfloat16

GRID_W = 64
N_DIFF_HEADS = 8
DIFF_QK_DIM = 64
DIFF_V_DIM = 128
DIFF_W = N_DIFF_HEADS * DIFF_V_DIM
N_NA_HEADS = 8
NA_HEAD_DIM = 128
NA_W = N_NA_HEADS * NA_HEAD_DIM
NA_WIN_ROWS = 8
NA_WIN_COLS = 16
ROPE_THETA = 10000.0
EPS = 1e-6
MASK_VALUE = -1e30
LOG2_E = math.log2(math.e)

LANES = 128
VMEM_LIMIT_BYTES = 56 * 1024 * 1024

MM_TM = 1024
MM_TN = 1024
MLP_OUT_TK = 512
ATT_TQ = 256
ATT_TK = 512
ATT_SCORE_BUFFERS = 3


def _compiler_params(semantics):
    return pltpu.CompilerParams(dimension_semantics=semantics,
                                vmem_limit_bytes=VMEM_LIMIT_BYTES)


def _rmsnorm_rows(x, g):
    return x * lax.rsqrt(jnp.mean(x * x, axis=-1, keepdims=True) + EPS) * g


def _norm_matmul_body(x_ref, g_ref, w_ref, o_ref, h_ref, acc_epilogue):
    @pl.when(pl.program_id(1) == 0)
    def _():
        h_ref[...] = _rmsnorm_rows(x_ref[...], g_ref[...]).astype(BF16)

    acc = jnp.dot(h_ref[...], w_ref[...], preferred_element_type=F32)
    acc_epilogue(acc, o_ref)


def _in_proj_kernel(x_ref, g_ref, w_ref, cos_ref, sin_lo_ref, sin_hi_ref, o_ref, h_ref):
    j = pl.program_id(1)
    n_rope_blocks = 2 * DIFF_W // MM_TN
    n_q_blocks = DIFF_W // MM_TN

    def epilogue(acc, o_ref):
        @pl.when(j < n_rope_blocks)
        def _():
            scale = jnp.where(j < n_q_blocks, LOG2_E * DIFF_QK_DIM ** -0.5, 1.0).astype(F32)
            cos = cos_ref[...]
            sin_lo = sin_lo_ref[...]
            sin_hi = sin_hi_ref[...]
            for c in range(MM_TN // LANES):
                xs = acc[:, c * LANES:(c + 1) * LANES]
                r = (xs * cos
                     + pltpu.roll(xs, LANES - DIFF_QK_DIM // 2, 1) * sin_lo
                     + pltpu.roll(xs, DIFF_QK_DIM // 2, 1) * sin_hi)
                o_ref[:, c * LANES:(c + 1) * LANES] = (r * scale).astype(o_ref.dtype)

        is_na_q = (j >= 3 * DIFF_W // MM_TN) & (j < (3 * DIFF_W + NA_W) // MM_TN)

        @pl.when(is_na_q)
        def _():
            o_ref[...] = (acc * (LOG2_E * NA_HEAD_DIM ** -0.5)).astype(o_ref.dtype)

        @pl.when((j >= n_rope_blocks) & jnp.logical_not(is_na_q))
        def _():
            o_ref[...] = acc.astype(o_ref.dtype)

    _norm_matmul_body(x_ref, g_ref, w_ref, o_ref, h_ref, epilogue)


def _mlp_in_kernel(x_ref, g_ref, w_ref, o_ref, h_ref):
    def epilogue(acc, o_ref):
        u = jnp.maximum(acc, 0.0)
        o_ref[...] = (u * u).astype(o_ref.dtype)

    _norm_matmul_body(x_ref, g_ref, w_ref, o_ref, h_ref, epilogue)


def _norm_matmul(kernel_fn, x, g, w, layer, extra=(), name=None):
    s, d = x.shape
    n = w.shape[2]
    tm = min(MM_TM, s)
    grid = (s // tm, n // MM_TN)
    extra_specs = [pl.BlockSpec((tm, LANES), lambda i, j: (i, 0)) for _ in extra]
    return pl.pallas_call(
        kernel_fn,
        grid=grid,
        in_specs=[pl.BlockSpec((tm, d), lambda i, j: (i, 0)),
                  pl.BlockSpec((1, d), lambda i, j: (0, 0)),
                  pl.BlockSpec((None, d, MM_TN), lambda i, j: (layer, 0, j))] + extra_specs,
        out_specs=pl.BlockSpec((tm, MM_TN), lambda i, j: (i, j)),
        out_shape=jax.ShapeDtypeStruct((s, n), BF16),
        scratch_shapes=[pltpu.VMEM((tm, d), BF16)],
        compiler_params=_compiler_params(("arbitrary", "arbitrary")),
        name=name,
    )(x, g.reshape(1, d), w, *extra)


def _diff_attn_kernel(lam_ref, q_ref, k_ref, v_ref, lq1_ref, lk1_ref, lq2_ref, lk2_ref,
                      g_ref, o_ref, vt_ref, qpad_ref, s_ref, acc_ref,
                      *, tq, tk, n_chunks):
    @pl.when(pl.program_id(1) == 0)
    def _():
        for c in range(n_chunks):
            vt_ref[c] = v_ref[c * tk:(c + 1) * tk, :].astype(F32).T.astype(BF16)

    qt = q_ref[...].astype(F32).T
    row = lax.broadcasted_iota(jnp.int32, qt.shape, 0)
    qpad_ref[...] = jnp.concatenate([jnp.where(row < DIFF_QK_DIM, qt, 0.0),
                                     jnp.where(row >= DIFF_QK_DIM, qt, 0.0)],
                                    axis=1).astype(BF16)
    acc_ref[...] = jnp.zeros_like(acc_ref)

    def scores(c):
        s = jnp.dot(k_ref[c * tk:(c + 1) * tk, :], qpad_ref[...],
                    preferred_element_type=F32)
        s_ref[c % ATT_SCORE_BUFFERS] = s
        return jnp.max(s, axis=0, keepdims=True)

    def update(c, cmax, m, l):
        m_new = jnp.maximum(m, cmax)
        alpha = jnp.exp2(m - m_new)
        p = jnp.exp2(s_ref[c % ATT_SCORE_BUFFERS] - m_new)
        l = alpha * l + jnp.sum(p, axis=0, keepdims=True)
        pv = jnp.dot(vt_ref[c], p.astype(BF16), preferred_element_type=F32)
        acc_ref[...] = acc_ref[...] * alpha + pv
        return m_new, l

    m = jnp.full((1, 2 * tq), MASK_VALUE, F32)
    l = jnp.zeros((1, 2 * tq), F32)
    lookahead = ATT_SCORE_BUFFERS - 1
    cmax = {c: scores(c) for c in range(min(lookahead, n_chunks))}
    for c in range(n_chunks):
        if c + lookahead < n_chunks:
            cmax[c + lookahead] = scores(c + lookahead)
        m, l = update(c, cmax.pop(c), m, l)

    lambda_init = lam_ref[0]
    lam = (jnp.exp(jnp.sum(lq1_ref[...] * lk1_ref[...], axis=-1, keepdims=True))
           - jnp.exp(jnp.sum(lq2_ref[...] * lk2_ref[...], axis=-1, keepdims=True))
           + lambda_init)
    o = acc_ref[...] / l
    od = (o[:, :tq] - lam * o[:, tq:]).T
    y = _rmsnorm_rows(od, g_ref[...]) * (1.0 - lambda_init)
    o_ref[...] = y.astype(o_ref.dtype)


def _diff_attention(proj, lam_init, lq1, lk1, lq2, lk2, subln_g):
    s = proj.shape[0]
    tq = min(ATT_TQ, s)
    tk = min(ATT_TK, s)
    n_chunks = s // tk
    kernel_fn = functools.partial(_diff_attn_kernel, tq=tq, tk=tk, n_chunks=n_chunks)
    k_col0 = DIFF_W // LANES
    v_col0 = 2 * DIFF_W // LANES
    vec = lambda n: pl.BlockSpec((1, n), lambda h, i: (0, 0))
    return pl.pallas_call(
        kernel_fn,
        grid=(N_DIFF_HEADS, s // tq),
        in_specs=[pl.BlockSpec(memory_space=pltpu.SMEM),
                  pl.BlockSpec((tq, LANES), lambda h, i: (i, h)),
                  pl.BlockSpec((s, LANES), lambda h, i: (0, k_col0 + h)),
                  pl.BlockSpec((s, LANES), lambda h, i: (0, v_col0 + h)),
                  vec(DIFF_QK_DIM), vec(DIFF_QK_DIM), vec(DIFF_QK_DIM), vec(DIFF_QK_DIM),
                  vec(DIFF_V_DIM)],
        out_specs=pl.BlockSpec((tq, LANES), lambda h, i: (i, h)),
        out_shape=jax.ShapeDtypeStruct((s, DIFF_W), BF16),
        scratch_shapes=[pltpu.VMEM((n_chunks, DIFF_V_DIM, tk), BF16),
                        pltpu.VMEM((LANES, 2 * tq), BF16),
                        pltpu.VMEM((ATT_SCORE_BUFFERS, tk, 2 * tq), F32),
                        pltpu.VMEM((DIFF_V_DIM, 2 * tq), F32)],
        compiler_params=_compiler_params(("arbitrary", "arbitrary")),
        name="diff_attention",
    )(lam_init, proj, proj, proj,
      lq1.reshape(1, -1), lk1.reshape(1, -1), lq2.reshape(1, -1), lk2.reshape(1, -1),
      subln_g.reshape(1, -1))


NA_GROUP = 4
NA_SPAN = 3 * NA_GROUP


def _na_span_start(g, n_groups):
    return jnp.clip(g - 1, 0, n_groups - 3)


def _na_attn_kernel(q_ref, k0_ref, k1_ref, k2_ref, v0_ref, v1_ref, v2_ref, bias_ref, g_ref,
                    o_ref, kwin_ref, vwin_ref, o_scr):
    blk = NA_GROUP * GRID_W
    for t, (k_ref, v_ref) in enumerate(((k0_ref, v0_ref), (k1_ref, v1_ref), (k2_ref, v2_ref))):
        kwin_ref[t * blk:(t + 1) * blk, :] = k_ref[...]
        vwin_ref[t * blk:(t + 1) * blk, :] = v_ref[...]
    for h in range(N_NA_HEADS):
        cols = slice(h * NA_HEAD_DIM, (h + 1) * NA_HEAD_DIM)
        sc = lax.dot_general(q_ref[:, cols], kwin_ref[:, cols], (((1,), (1,)), ((), ())),
                             preferred_element_type=F32)
        sc = sc + bias_ref[h]
        p = jnp.exp2(sc - jnp.max(sc, axis=-1, keepdims=True))
        l = jnp.sum(p, axis=-1, keepdims=True)
        pv = jnp.dot(p.astype(BF16), vwin_ref[:, cols], preferred_element_type=F32)
        o_scr[:, cols] = pv / l
    o_ref[...] = _rmsnorm_rows(o_scr[...], g_ref[...]).astype(o_ref.dtype)


def _na_window_plan(rows):
    n_groups = rows // NA_GROUP
    kr = NA_WIN_ROWS

    def plan(g):
        start = NA_GROUP * int(np.clip(g - 1, 0, n_groups - 3))
        out = np.full((NA_GROUP, NA_SPAN), -1, np.int64)
        for j in range(NA_GROUP):
            r = NA_GROUP * g + j
            rs = int(np.clip(r - kr // 2, 0, rows - kr))
            for u in range(NA_SPAN):
                if rs <= start + u < rs + kr:
                    out[j, u] = start + u - r + NA_WIN_ROWS - 1
        assert (out >= 0).sum() == NA_GROUP * kr
        return out

    plans = [plan(0), plan(1), plan(n_groups - 1)]
    for g in range(1, n_groups - 1):
        assert (plan(g) == plans[1]).all()
    return plans


def _na_bias_table(rpb, rows):
    c = np.arange(GRID_W)
    cs = np.clip(c - NA_WIN_COLS // 2, 0, GRID_W - NA_WIN_COLS)
    kc = np.arange(GRID_W)
    col_valid = (kc[None, :] >= cs[:, None]) & (kc[None, :] < cs[:, None] + NA_WIN_COLS)
    pad = GRID_W - NA_WIN_COLS
    rp = jnp.pad(rpb.astype(F32) * LOG2_E, ((0, 0), (0, 0), (pad, pad)))
    by_col = jnp.stack([lax.slice_in_dim(rp, NA_WIN_COLS - 1 - ci + pad,
                                         NA_WIN_COLS - 1 - ci + pad + GRID_W, axis=2)
                        for ci in range(GRID_W)], axis=2)
    by_col = jnp.where(col_valid[None, None], by_col, MASK_VALUE)
    masked = jnp.full((N_NA_HEADS, GRID_W, GRID_W), MASK_VALUE, F32)
    variants = []
    for plan in _na_window_plan(rows):
        rows_j = []
        for j in range(NA_GROUP):
            tiles = [by_col[:, plan[j, u]] if plan[j, u] >= 0 else masked
                     for u in range(NA_SPAN)]
            rows_j.append(jnp.concatenate(tiles, axis=-1))
        variants.append(jnp.concatenate(rows_j, axis=1))
    return jnp.stack(variants)


def _na_attention(proj, rpb, na_g):
    s = proj.shape[0]
    rows = s // GRID_W
    assert rows % NA_GROUP == 0 and rows >= NA_SPAN and NA_SPAN >= NA_GROUP + NA_WIN_ROWS - 1
    n_groups = rows // NA_GROUP
    blk = NA_GROUP * GRID_W
    bias = _na_bias_table(rpb, rows)
    q_col = 3 * DIFF_W // NA_W
    k_col, v_col = q_col + 1, q_col + 2

    def kv_spec(t, col):
        return pl.BlockSpec((blk, NA_W), lambda g: (_na_span_start(g, n_groups) + t, col))

    def bias_variant(g):
        return jnp.where(g == 0, 0, jnp.where(g == n_groups - 1, 2, 1))

    return pl.pallas_call(
        _na_attn_kernel,
        grid=(n_groups,),
        in_specs=[pl.BlockSpec((blk, NA_W), lambda g: (g, q_col))]
                 + [kv_spec(t, k_col) for t in range(3)]
                 + [kv_spec(t, v_col) for t in range(3)]
                 + [pl.BlockSpec((None, N_NA_HEADS, blk, NA_SPAN * GRID_W),
                                 lambda g: (bias_variant(g), 0, 0, 0)),
                    pl.BlockSpec((1, NA_W), lambda g: (0, 0))],
        out_specs=pl.BlockSpec((blk, NA_W), lambda g: (g, 0)),
        out_shape=jax.ShapeDtypeStruct((s, NA_W), BF16),
        scratch_shapes=[pltpu.VMEM((NA_SPAN * GRID_W, NA_W), BF16),
                        pltpu.VMEM((NA_SPAN * GRID_W, NA_W), BF16),
                        pltpu.VMEM((blk, NA_W), F32)],
        compiler_params=_compiler_params(("arbitrary",)),
        name="na_attention",
    )(*([proj] * 7), bias, na_g.reshape(1, -1))


def _out_proj_kernel(a_ref, b_ref, w_ref, x_ref, o_ref):
    ka = a_ref.shape[1]
    acc = jnp.dot(a_ref[...], w_ref[:ka, :], preferred_element_type=F32)
    acc += jnp.dot(b_ref[...], w_ref[ka:, :], preferred_element_type=F32)
    o_ref[...] = x_ref[...] + acc


def _out_proj(o_diff, o_na, w, layer, x):
    s, d = x.shape
    tm = min(MM_TM, s)
    ka, kb = o_diff.shape[1], o_na.shape[1]
    return pl.pallas_call(
        _out_proj_kernel,
        grid=(s // tm, d // MM_TN),
        in_specs=[pl.BlockSpec((tm, ka), lambda i, j: (i, 0)),
                  pl.BlockSpec((tm, kb), lambda i, j: (i, 0)),
                  pl.BlockSpec((None, ka + kb, MM_TN), lambda i, j: (layer, 0, j)),
                  pl.BlockSpec((tm, MM_TN), lambda i, j: (i, j))],
        out_specs=pl.BlockSpec((tm, MM_TN), lambda i, j: (i, j)),
        out_shape=jax.ShapeDtypeStruct((s, d), F32),
        compiler_params=_compiler_params(("arbitrary", "arbitrary")),
        name="out_proj",
    )(o_diff, o_na, w, x)


def _mlp_out_kernel(u_ref, w_ref, x_ref, *rest, final_norm):
    o_ref = rest[-1]
    k = pl.program_id(1)

    @pl.when(k == 0)
    def _():
        o_ref[...] = x_ref[...]

    o_ref[...] += jnp.dot(u_ref[...], w_ref[...], preferred_element_type=F32)

    if final_norm:
        @pl.when(k == pl.num_programs(1) - 1)
        def _():
            o_ref[...] = _rmsnorm_rows(o_ref[...], rest[0][...])


def _mlp_out(u, w, layer, x, final_g=None):
    s, d = x.shape
    f = u.shape[1]
    tm = min(MM_TM, s)
    extra, extra_specs = (), []
    if final_g is not None:
        extra = (final_g.reshape(1, d),)
        extra_specs = [pl.BlockSpec((1, d), lambda i, k: (0, 0))]
    return pl.pallas_call(
        functools.partial(_mlp_out_kernel, final_norm=final_g is not None),
        grid=(s // tm, f // MLP_OUT_TK),
        in_specs=[pl.BlockSpec((tm, MLP_OUT_TK), lambda i, k: (i, k)),
                  pl.BlockSpec((None, MLP_OUT_TK, d), lambda i, k: (layer, k, 0)),
                  pl.BlockSpec((tm, d), lambda i, k: (i, 0))] + extra_specs,
        out_specs=pl.BlockSpec((tm, d), lambda i, k: (i, 0)),
        out_shape=jax.ShapeDtypeStruct((s, d), F32),
        compiler_params=_compiler_params(("arbitrary", "arbitrary")),
        name="mlp_out_final" if final_g is not None else "mlp_out",
    )(u, w, x, *extra)


def _rope_tables(seq):
    inv_freq = 1.0 / (ROPE_THETA ** (jnp.arange(0, DIFF_QK_DIM, 2, dtype=F32) / DIFF_QK_DIM))
    ang = jnp.arange(seq, dtype=F32)[:, None] * inv_freq[None, :]
    cos, sin = jnp.cos(ang), jnp.sin(ang)
    zero = jnp.zeros_like(sin)
    reps = LANES // DIFF_QK_DIM
    cos_t = jnp.tile(cos, (1, 2 * reps))
    sin_lo = jnp.tile(jnp.concatenate([-sin, zero], axis=1), (1, reps))
    sin_hi = jnp.tile(jnp.concatenate([zero, sin], axis=1), (1, reps))
    return cos_t, sin_lo, sin_hi


def kernel(x, attn_norm, w_in, lambda_q1, lambda_k1, lambda_q2, lambda_k2, diff_subln, na_norm,
           na_rpb, w_out, mlp_norm, w_mlp_in, w_mlp_out, final_norm):
    b, s, d = x.shape
    depth = w_in.shape[0]
    rope = _rope_tables(s)
    w_in, w_out, w_mlp_in, w_mlp_out = (w.astype(BF16) for w in (w_in, w_out, w_mlp_in, w_mlp_out))
    outs = []
    for bi in range(b):
        xb = x[bi].astype(F32)
        for l in range(depth):
            lambda_init = 0.8 - 0.6 * math.exp(-0.3 * l)
            proj = _norm_matmul(_in_proj_kernel, xb, attn_norm[l].astype(F32), w_in, l,
                                rope, name="in_proj")
            o_diff = _diff_attention(proj, jnp.full((1,), lambda_init, F32),
                                     lambda_q1[l].astype(F32), lambda_k1[l].astype(F32),
                                     lambda_q2[l].astype(F32), lambda_k2[l].astype(F32),
                                     diff_subln[l].astype(F32))
            o_na = _na_attention(proj, na_rpb[l], na_norm[l].astype(F32))
            xb = _out_proj(o_diff, o_na, w_out, l, xb)
            u = _norm_matmul(_mlp_in_kernel, xb, mlp_norm[l].astype(F32), w_mlp_in, l,
                             name="mlp_in")
            xb = _mlp_out(u, w_mlp_out, l, xb,
                          final_norm.astype(F32) if l == depth - 1 else None)
        outs.append(xb)
    out = outs[0][None] if b == 1 else jnp.stack(outs)
    return out.astype(x.dtype)
```

```python
import functools
import math

import jax
import jax.numpy as jnp
import numpy as np
from jax import lax
from jax.experimental import pallas as pl
from jax.experimental.pallas import tpu as pltpu

F32 = jnp.float32
BF16 = jnp.bfloat16

GRID_W = 64
N_DIFF_HEADS = 8
DIFF_QK_DIM = 64
DIFF_V_DIM = 128
DIFF_W = N_DIFF_HEADS * DIFF_V_DIM
N_NA_HEADS = 8
NA_HEAD_DIM = 128
NA_W = N_NA_HEADS * NA_HEAD_DIM
NA_WIN_ROWS = 8
NA_WIN_COLS = 16
ROPE_THETA = 10000.0
EPS = 1e-6
MASK_VALUE = -1e30
LOG2_E = math.log2(math.e)

LANES = 128
VMEM_LIMIT_BYTES = 56 * 1024 * 1024

MM_TM = 1024
MM_TN = 1024
MM_ROW_SPLIT = 4
MLP_OUT_TK = 1024
ATT_TQ = 256
ATT_TK = 512
ATT_SCORE_BUFFERS = 4


def _compiler_params(semantics, flags=None):
    return pltpu.CompilerParams(dimension_semantics=semantics,
                                vmem_limit_bytes=VMEM_LIMIT_BYTES, flags=flags)


def _rmsnorm_rows(x, g):
    return x * lax.rsqrt(jnp.mean(x * x, axis=-1, keepdims=True) + EPS) * g


def _norm_rows_once(x_ref, g_ref, h_ref):
    @pl.when(pl.program_id(1) == 0)
    def _():
        h_ref[...] = _rmsnorm_rows(x_ref[...], g_ref[...]).astype(BF16)


def _matmul_row_blocks(h_ref, w_ref, epilogue):
    tr = h_ref.shape[0] // MM_ROW_SPLIT
    for r in range(MM_ROW_SPLIT):
        rows = slice(r * tr, (r + 1) * tr)
        epilogue(rows, jnp.dot(h_ref[rows, :], w_ref[...], preferred_element_type=F32))


def _in_proj_kernel(x_ref, g_ref, w_ref, cos_ref, sin_lo_ref, sin_hi_ref, o_ref, h_ref):
    j = pl.program_id(1)
    n_rope_blocks = 2 * DIFF_W // MM_TN
    n_q_blocks = DIFF_W // MM_TN
    _norm_rows_once(x_ref, g_ref, h_ref)

    @pl.when(j < n_rope_blocks)
    def _():
        scale = jnp.where(j < n_q_blocks, LOG2_E * DIFF_QK_DIM ** -0.5, 1.0).astype(F32)

        def rope(rows, acc):
            cos = cos_ref[rows, :] * scale
            sin_lo = sin_lo_ref[rows, :] * scale
            sin_hi = sin_hi_ref[rows, :] * scale
            for c in range(MM_TN // LANES):
                xs = acc[:, c * LANES:(c + 1) * LANES]
                r = (xs * cos
                     + pltpu.roll(xs, LANES - DIFF_QK_DIM // 2, 1) * sin_lo
                     + pltpu.roll(xs, DIFF_QK_DIM // 2, 1) * sin_hi)
                o_ref[rows, c * LANES:(c + 1) * LANES] = r.astype(o_ref.dtype)

        _matmul_row_blocks(h_ref, w_ref, rope)

    @pl.when(j >= n_rope_blocks)
    def _():
        is_na_q = (j >= 3 * DIFF_W // MM_TN) & (j < (3 * DIFF_W + NA_W) // MM_TN)
        scale = jnp.where(is_na_q, LOG2_E * NA_HEAD_DIM ** -0.5, 1.0).astype(F32)

        def scaled(rows, acc):
            o_ref[rows, :] = (acc * scale).astype(o_ref.dtype)

        _matmul_row_blocks(h_ref, w_ref, scaled)


def _mlp_in_kernel(x_ref, g_ref, w_ref, o_ref, h_ref):
    _norm_rows_once(x_ref, g_ref, h_ref)

    def relu2(rows, acc):
        u = jnp.maximum(acc, 0.0)
        o_ref[rows, :] = (u * u).astype(o_ref.dtype)

    _matmul_row_blocks(h_ref, w_ref, relu2)


def _norm_matmul(kernel_fn, x, g, w, layer, extra=(), name=None):
    s, d = x.shape
    n = w.shape[2]
    tm = min(MM_TM, s)
    grid = (s // tm, n // MM_TN)
    extra_specs = [pl.BlockSpec((tm, LANES), lambda i, j: (i, 0)) for _ in extra]
    return pl.pallas_call(
        kernel_fn,
        grid=grid,
        in_specs=[pl.BlockSpec((tm, d), lambda i, j: (i, 0)),
                  pl.BlockSpec((1, d), lambda i, j: (0, 0)),
                  pl.BlockSpec((None, d, MM_TN), lambda i, j: (layer, 0, j))] + extra_specs,
        out_specs=pl.BlockSpec((tm, MM_TN), lambda i, j: (i, j)),
        out_shape=jax.ShapeDtypeStruct((s, n), BF16),
        scratch_shapes=[pltpu.VMEM((tm, d), BF16)],
        compiler_params=_compiler_params(("arbitrary", "arbitrary")),
        name=name,
    )(x, g.reshape(1, d), w, *extra)


def _diff_attn_kernel(lam_ref, q_ref, k_ref, v_ref, lq1_ref, lk1_ref, lq2_ref, lk2_ref,
                      g_ref, o_ref, vt_ref, qpad_ref, s_ref, acc_ref,
                      *, tq, tk, n_chunks, n_blocks):
    for c in range(n_chunks):
        vt_ref[c] = v_ref[c * tk:(c + 1) * tk, :].astype(F32).T.astype(BF16)

    lookahead = ATT_SCORE_BUFFERS - 2
    lambda_init = lam_ref[0]
    lam = (jnp.exp(jnp.sum(lq1_ref[...] * lk1_ref[...], axis=-1, keepdims=True))
           - jnp.exp(jnp.sum(lq2_ref[...] * lk2_ref[...], axis=-1, keepdims=True))
           + lambda_init)

    def load_queries(i):
        qt = q_ref[pl.ds(pl.multiple_of(i * tq, tq), tq), :].astype(F32).T
        row = lax.broadcasted_iota(jnp.int32, qt.shape, 0)
        qpad_ref[...] = jnp.concatenate([jnp.where(row < DIFF_QK_DIM, qt, 0.0),
                                         jnp.where(row >= DIFF_QK_DIM, qt, 0.0)],
                                        axis=1).astype(BF16)

    def scores(c):
        s = jnp.dot(k_ref[c * tk:(c + 1) * tk, :], qpad_ref[...],
                    preferred_element_type=F32)
        s_ref[c % ATT_SCORE_BUFFERS] = s
        return jnp.max(s, axis=0, keepdims=True)

    def update(c, cmax, m, l):
        m_new = jnp.maximum(m, cmax)
        alpha = jnp.exp2(m - m_new)
        p = jnp.exp2(s_ref[c % ATT_SCORE_BUFFERS] - m_new)
        l = alpha * l + jnp.sum(p, axis=0, keepdims=True)
        pv = jnp.dot(vt_ref[c], p.astype(BF16), preferred_element_type=F32)
        acc_ref[...] = acc_ref[...] * alpha + pv
        return m_new, l

    def start_block(i):
        load_queries(i)
        return tuple(scores(c) for c in range(lookahead))

    def block(i, first_cmax):
        acc_ref[...] = jnp.zeros_like(acc_ref)
        m = jnp.full((1, 2 * tq), MASK_VALUE, F32)
        l = jnp.zeros((1, 2 * tq), F32)
        cmax = dict(enumerate(first_cmax))
        for c in range(n_chunks):
            if c + lookahead < n_chunks:
                cmax[c + lookahead] = scores(c + lookahead)
            m, l = update(c, cmax.pop(c), m, l)
        next_cmax = start_block(jnp.minimum(i + 1, n_blocks - 1))

        o = acc_ref[...] / l
        od = (o[:, :tq] - lam * o[:, tq:]).T
        y = _rmsnorm_rows(od, g_ref[...]) * (1.0 - lambda_init)
        o_ref[pl.ds(pl.multiple_of(i * tq, tq), tq), :] = y.astype(o_ref.dtype)
        return next_cmax

    lax.fori_loop(0, n_blocks, block, start_block(0))


def _diff_attention(proj, lam_init, lq1, lk1, lq2, lk2, subln_g):
    s = proj.shape[0]
    tq = min(ATT_TQ, s)
    tk = min(ATT_TK, s)
    n_chunks = s // tk
    assert n_chunks % ATT_SCORE_BUFFERS == 0, "score slots must line up across query blocks"
    kernel_fn = functools.partial(_diff_attn_kernel, tq=tq, tk=tk, n_chunks=n_chunks,
                                  n_blocks=s // tq)
    k_col0 = DIFF_W // LANES
    v_col0 = 2 * DIFF_W // LANES
    vec = lambda n: pl.BlockSpec((1, n), lambda h: (0, 0))
    return pl.pallas_call(
        kernel_fn,
        grid=(N_DIFF_HEADS,),
        in_specs=[pl.BlockSpec(memory_space=pltpu.SMEM),
                  pl.BlockSpec((s, LANES), lambda h: (0, h)),
                  pl.BlockSpec((s, LANES), lambda h: (0, k_col0 + h)),
                  pl.BlockSpec((s, LANES), lambda h: (0, v_col0 + h)),
                  vec(DIFF_QK_DIM), vec(DIFF_QK_DIM), vec(DIFF_QK_DIM), vec(DIFF_QK_DIM),
                  vec(DIFF_V_DIM)],
        out_specs=pl.BlockSpec((s, LANES), lambda h: (0, h)),
        out_shape=jax.ShapeDtypeStruct((s, DIFF_W), BF16),
        scratch_shapes=[pltpu.VMEM((n_chunks, DIFF_V_DIM, tk), BF16),
                        pltpu.VMEM((LANES, 2 * tq), BF16),
                        pltpu.VMEM((ATT_SCORE_BUFFERS, tk, 2 * tq), F32),
                        pltpu.VMEM((DIFF_V_DIM, 2 * tq), F32)],
        compiler_params=_compiler_params(("arbitrary",)),
        name="diff_attention",
    )(lam_init, proj, proj, proj,
      lq1.reshape(1, -1), lk1.reshape(1, -1), lq2.reshape(1, -1), lk2.reshape(1, -1),
      subln_g.reshape(1, -1))


NA_GROUP = 4
NA_SPAN = 3 * NA_GROUP


def _na_span_start(g, n_groups):
    return jnp.clip(g - 1, 0, n_groups - 3)


def _na_attn_kernel(q_ref, k0_ref, k1_ref, k2_ref, v0_ref, v1_ref, v2_ref, bias_ref, g_ref,
                    o_ref, kwin_ref, vwin_ref, o_scr):
    blk = NA_GROUP * GRID_W
    for t, (k_ref, v_ref) in enumerate(((k0_ref, v0_ref), (k1_ref, v1_ref), (k2_ref, v2_ref))):
        kwin_ref[t * blk:(t + 1) * blk, :] = k_ref[...]
        vwin_ref[t * blk:(t + 1) * blk, :] = v_ref[...]
    for h in range(N_NA_HEADS):
        cols = slice(h * NA_HEAD_DIM, (h + 1) * NA_HEAD_DIM)
        sc = lax.dot_general(q_ref[:, cols], kwin_ref[:, cols], (((1,), (1,)), ((), ())),
                             preferred_element_type=F32)
        sc = sc + bias_ref[h]
        p = jnp.exp2(sc - jnp.max(sc, axis=-1, keepdims=True))
        l = jnp.sum(p, axis=-1, keepdims=True)
        pv = jnp.dot(p.astype(BF16), vwin_ref[:, cols], preferred_element_type=F32)
        o_scr[:, cols] = pv / l
    o_ref[...] = _rmsnorm_rows(o_scr[...], g_ref[...]).astype(o_ref.dtype)


def _na_window_plan(rows):
    n_groups = rows // NA_GROUP
    kr = NA_WIN_ROWS

    def plan(g):
        start = NA_GROUP * int(np.clip(g - 1, 0, n_groups - 3))
        out = np.full((NA_GROUP, NA_SPAN), -1, np.int64)
        for j in range(NA_GROUP):
            r = NA_GROUP * g + j
            rs = int(np.clip(r - kr // 2, 0, rows - kr))
            for u in range(NA_SPAN):
                if rs <= start + u < rs + kr:
                    out[j, u] = start + u - r + NA_WIN_ROWS - 1
        assert (out >= 0).sum() == NA_GROUP * kr
        return out

    plans = [plan(0), plan(1), plan(n_groups - 1)]
    for g in range(1, n_groups - 1):
        assert (plan(g) == plans[1]).all()
    return plans


def _na_bias_table(rpb, rows):
    c = np.arange(GRID_W)
    cs = np.clip(c - NA_WIN_COLS // 2, 0, GRID_W - NA_WIN_COLS)
    kc = np.arange(GRID_W)
    col_valid = (kc[None, :] >= cs[:, None]) & (kc[None, :] < cs[:, None] + NA_WIN_COLS)
    pad = GRID_W - NA_WIN_COLS
    rp = jnp.pad(rpb.astype(F32) * LOG2_E, ((0, 0), (0, 0), (pad, pad)))
    by_col = jnp.stack([lax.slice_in_dim(rp, NA_WIN_COLS - 1 - ci + pad,
                                         NA_WIN_COLS - 1 - ci + pad + GRID_W, axis=2)
                        for ci in range(GRID_W)], axis=2)
    by_col = jnp.where(col_valid[None, None], by_col, MASK_VALUE)
    masked = jnp.full((N_NA_HEADS, GRID_W, GRID_W), MASK_VALUE, F32)
    variants = []
    for plan in _na_window_plan(rows):
        rows_j = []
        for j in range(NA_GROUP):
            tiles = [by_col[:, plan[j, u]] if plan[j, u] >= 0 else masked
                     for u in range(NA_SPAN)]
            rows_j.append(jnp.concatenate(tiles, axis=-1))
        variants.append(jnp.concatenate(rows_j, axis=1))
    return jnp.stack(variants)


def _na_attention(proj, rpb, na_g):
    s = proj.shape[0]
    rows = s // GRID_W
    assert rows % NA_GROUP == 0 and rows >= NA_SPAN and NA_SPAN >= NA_GROUP + NA_WIN_ROWS - 1
    n_groups = rows // NA_GROUP
    blk = NA_GROUP * GRID_W
    bias = _na_bias_table(rpb, rows)
    q_col = 3 * DIFF_W // NA_W
    k_col, v_col = q_col + 1, q_col + 2

    def kv_spec(t, col):
        return pl.BlockSpec((blk, NA_W), lambda g: (_na_span_start(g, n_groups) + t, col))

    def bias_variant(g):
        return jnp.where(g == 0, 0, jnp.where(g == n_groups - 1, 2, 1))

    return pl.pallas_call(
        _na_attn_kernel,
        grid=(n_groups,),
        in_specs=[pl.BlockSpec((blk, NA_W), lambda g: (g, q_col))]
                 + [kv_spec(t, k_col) for t in range(3)]
                 + [kv_spec(t, v_col) for t in range(3)]
                 + [pl.BlockSpec((None, N_NA_HEADS, blk, NA_SPAN * GRID_W),
                                 lambda g: (bias_variant(g), 0, 0, 0)),
                    pl.BlockSpec((1, NA_W), lambda g: (0, 0))],
        out_specs=pl.BlockSpec((blk, NA_W), lambda g: (g, 0)),
        out_shape=jax.ShapeDtypeStruct((s, NA_W), BF16),
        scratch_shapes=[pltpu.VMEM((NA_SPAN * GRID_W, NA_W), BF16),
                        pltpu.VMEM((NA_SPAN * GRID_W, NA_W), BF16),
                        pltpu.VMEM((blk, NA_W), F32)],
        compiler_params=_compiler_params(("arbitrary",)),
        name="na_attention",
    )(*([proj] * 7), bias, na_g.reshape(1, -1))


def _out_proj_kernel(a_ref, b_ref, w_ref, x_ref, o_ref):
    ka = a_ref.shape[1]
    acc = jnp.dot(a_ref[...], w_ref[:ka, :], preferred_element_type=F32)
    acc += jnp.dot(b_ref[...], w_ref[ka:, :], preferred_element_type=F32)
    o_ref[...] = x_ref[...] + acc


def _out_proj(o_diff, o_na, w, layer, x):
    s, d = x.shape
    tm = min(MM_TM, s)
    ka, kb = o_diff.shape[1], o_na.shape[1]
    return pl.pallas_call(
        _out_proj_kernel,
        grid=(s // tm, d // MM_TN),
        in_specs=[pl.BlockSpec((tm, ka), lambda i, j: (i, 0)),
                  pl.BlockSpec((tm, kb), lambda i, j: (i, 0)),
                  pl.BlockSpec((None, ka + kb, MM_TN), lambda i, j: (layer, 0, j)),
                  pl.BlockSpec((tm, MM_TN), lambda i, j: (i, j))],
        out_specs=pl.BlockSpec((tm, MM_TN), lambda i, j: (i, j)),
        out_shape=jax.ShapeDtypeStruct((s, d), F32),
        compiler_params=_compiler_params(("arbitrary", "arbitrary")),
        name="out_proj",
    )(o_diff, o_na, w, x)


def _mlp_out_kernel(u_ref, w_ref, x_ref, *rest, final_norm):
    o_ref = rest[-1]
    k = pl.program_id(1)

    @pl.when(k == 0)
    def _():
        o_ref[...] = x_ref[...]

    o_ref[...] += jnp.dot(u_ref[...], w_ref[...], preferred_element_type=F32)

    if final_norm:
        @pl.when(k == pl.num_programs(1) - 1)
        def _():
            o_ref[...] = _rmsnorm_rows(o_ref[...], rest[0][...])


def _mlp_out(u, w, layer, x, final_g=None):
    s, d = x.shape
    f = u.shape[1]
    tm = min(MM_TM, s)
    extra, extra_specs = (), []
    if final_g is not None:
        extra = (final_g.reshape(1, d),)
        extra_specs = [pl.BlockSpec((1, d), lambda i, k: (0, 0))]
    return pl.pallas_call(
        functools.partial(_mlp_out_kernel, final_norm=final_g is not None),
        grid=(s // tm, f // MLP_OUT_TK),
        in_specs=[pl.BlockSpec((tm, MLP_OUT_TK), lambda i, k: (i, k)),
                  pl.BlockSpec((None, MLP_OUT_TK, d), lambda i, k: (layer, k, 0)),
                  pl.BlockSpec((tm, d), lambda i, k: (i, 0))] + extra_specs,
        out_specs=pl.BlockSpec((tm, d), lambda i, k: (i, 0)),
        out_shape=jax.ShapeDtypeStruct((s, d), F32),
        compiler_params=_compiler_params(("arbitrary", "arbitrary")),
        name="mlp_out_final" if final_g is not None else "mlp_out",
    )(u, w, x, *extra)


def _rope_tables(seq):
    inv_freq = 1.0 / (ROPE_THETA ** (jnp.arange(0, DIFF_QK_DIM, 2, dtype=F32) / DIFF_QK_DIM))
    ang = jnp.arange(seq, dtype=F32)[:, None] * inv_freq[None, :]
    cos, sin = jnp.cos(ang), jnp.sin(ang)
    zero = jnp.zeros_like(sin)
    reps = LANES // DIFF_QK_DIM
    cos_t = jnp.tile(cos, (1, 2 * reps))
    sin_lo = jnp.tile(jnp.concatenate([-sin, zero], axis=1), (1, reps))
    sin_hi = jnp.tile(jnp.concatenate([zero, sin], axis=1), (1, reps))
    return cos_t, sin_lo, sin_hi


def kernel(x, attn_norm, w_in, lambda_q1, lambda_k1, lambda_q2, lambda_k2, diff_subln, na_norm,
           na_rpb, w_out, mlp_norm, w_mlp_in, w_mlp_out, final_norm):
    b, s, d = x.shape
    depth = w_in.shape[0]
    rope = _rope_tables(s)
    w_in, w_out, w_mlp_in, w_mlp_out = (w.astype(BF16) for w in (w_in, w_out, w_mlp_in, w_mlp_out))
    outs = []
    for bi in range(b):
        xb = x[bi].astype(F32)
        for l in range(depth):
            lambda_init = 0.8 - 0.6 * math.exp(-0.3 * l)
            proj = _norm_matmul(_in_proj_kernel, xb, attn_norm[l].astype(F32), w_in, l,
                                rope, name="in_proj")
            o_diff = _diff_attention(proj, jnp.full((1,), lambda_init, F32),
                                     lambda_q1[l].astype(F32), lambda_k1[l].astype(F32),
                                     lambda_q2[l].astype(F32), lambda_k2[l].astype(F32),
                                     diff_subln[l].astype(F32))
            o_na = _na_attention(proj, na_rpb[l], na_norm[l].astype(F32))
            xb = _out_proj(o_diff, o_na, w_out, l, xb)
            u = _norm_matmul(_mlp_in_kernel, xb, mlp_norm[l].astype(F32), w_mlp_in, l,
                             name="mlp_in")
            xb = _mlp_out(u, w_mlp_out, l, xb,
                          final_norm.astype(F32) if l == depth - 1 else None)
        outs.append(xb)
    out = outs[0][None] if b == 1 else jnp.stack(outs)
    return out.astype(x.dtype)
```

```python
import functools
import math

import jax
import jax.numpy as jnp
import numpy as np
from jax import lax
from jax.experimental import pallas as pl
from jax.experimental.pallas import tpu as pltpu

F32 = jnp.float32
BF16 = jnp.bfloat16

GRID_W = 64
N_DIFF_HEADS = 8
DIFF_QK_DIM = 64
DIFF_V_DIM = 128
DIFF_W = N_DIFF_HEADS * DIFF_V_DIM
N_NA_HEADS = 8
NA_HEAD_DIM = 128
NA_W = N_NA_HEADS * NA_HEAD_DIM
NA_WIN_ROWS = 8
NA_WIN_COLS = 16
ROPE_THETA = 10000.0
EPS = 1e-6
MASK_VALUE = -1e30
LOG2_E = math.log2(math.e)

LANES = 128
VMEM_LIMIT_BYTES = 56 * 1024 * 1024

MM_TM = 1024
MM_TN = 1024
MLP_IN_TN = 2048
OUT_PROJ_TM = 512
MM_ROW_SPLIT = 4
MLP_OUT_TK = 1024
ATT_TQ = 256
ATT_TK = 512
ATT_LOOKAHEAD = 2
ATT_SCORE_BUFFERS = 4


def _compiler_params(semantics, flags=None):
    return pltpu.CompilerParams(dimension_semantics=semantics,
                                vmem_limit_bytes=VMEM_LIMIT_BYTES, flags=flags)


def _rmsnorm_rows(x, g):
    return x * lax.rsqrt(jnp.mean(x * x, axis=-1, keepdims=True) + EPS) * g


def _norm_rows_once(x_ref, g_ref, h_ref):
    @pl.when(pl.program_id(1) == 0)
    def _():
        h_ref[...] = _rmsnorm_rows(x_ref[...], g_ref[...]).astype(BF16)


def _matmul_row_blocks(h_ref, w_ref, epilogue):
    tr = h_ref.shape[0] // MM_ROW_SPLIT
    for r in range(MM_ROW_SPLIT):
        rows = slice(r * tr, (r + 1) * tr)
        epilogue(rows, jnp.dot(h_ref[rows, :], w_ref[...], preferred_element_type=F32))


def _in_proj_kernel(x_ref, g_ref, w_ref, cos_ref, sin_lo_ref, sin_hi_ref, o_ref, h_ref):
    j = pl.program_id(1)
    n_rope_blocks = 2 * DIFF_W // MM_TN
    n_q_blocks = DIFF_W // MM_TN
    _norm_rows_once(x_ref, g_ref, h_ref)

    @pl.when(j < n_rope_blocks)
    def _():
        scale = jnp.where(j < n_q_blocks, LOG2_E * DIFF_QK_DIM ** -0.5, 1.0).astype(F32)

        def rope(rows, acc):
            cos = cos_ref[rows, :] * scale
            sin_lo = sin_lo_ref[rows, :] * scale
            sin_hi = sin_hi_ref[rows, :] * scale
            for c in range(MM_TN // LANES):
                xs = acc[:, c * LANES:(c + 1) * LANES]
                r = (xs * cos
                     + pltpu.roll(xs, LANES - DIFF_QK_DIM // 2, 1) * sin_lo
                     + pltpu.roll(xs, DIFF_QK_DIM // 2, 1) * sin_hi)
                o_ref[rows, c * LANES:(c + 1) * LANES] = r.astype(o_ref.dtype)

        _matmul_row_blocks(h_ref, w_ref, rope)

    @pl.when(j >= n_rope_blocks)
    def _():
        is_na_q = (j >= 3 * DIFF_W // MM_TN) & (j < (3 * DIFF_W + NA_W) // MM_TN)
        scale = jnp.where(is_na_q, LOG2_E * NA_HEAD_DIM ** -0.5, 1.0).astype(F32)

        def scaled(rows, acc):
            o_ref[rows, :] = (acc * scale).astype(o_ref.dtype)

        _matmul_row_blocks(h_ref, w_ref, scaled)


def _mlp_in_kernel(x_ref, g_ref, w_ref, o_ref, h_ref):
    _norm_rows_once(x_ref, g_ref, h_ref)

    def relu2(rows, acc):
        u = jnp.maximum(acc, 0.0)
        o_ref[rows, :] = (u * u).astype(o_ref.dtype)

    _matmul_row_blocks(h_ref, w_ref, relu2)


def _norm_matmul(kernel_fn, x, g, w, layer, tn, extra=(), name=None):
    s, d = x.shape
    n = w.shape[2]
    tm = min(MM_TM, s)
    grid = (s // tm, n // tn)
    extra_specs = [pl.BlockSpec((tm, LANES), lambda i, j: (i, 0)) for _ in extra]
    return pl.pallas_call(
        kernel_fn,
        grid=grid,
        in_specs=[pl.BlockSpec((tm, d), lambda i, j: (i, 0)),
                  pl.BlockSpec((1, d), lambda i, j: (0, 0)),
                  pl.BlockSpec((None, d, tn), lambda i, j: (layer, 0, j))] + extra_specs,
        out_specs=pl.BlockSpec((tm, tn), lambda i, j: (i, j)),
        out_shape=jax.ShapeDtypeStruct((s, n), BF16),
        scratch_shapes=[pltpu.VMEM((tm, d), BF16)],
        compiler_params=_compiler_params(("arbitrary", "arbitrary")),
        name=name,
    )(x, g.reshape(1, d), w, *extra)


def _diff_attn_kernel(lam_ref, q_ref, k_ref, v_ref, lq1_ref, lk1_ref, lq2_ref, lk2_ref,
                      g_ref, o_ref, vt_ref, qpad_ref, s_ref, acc_ref,
                      *, tq, tk, n_chunks, n_blocks):
    for c in range(n_chunks):
        vt_ref[c] = v_ref[c * tk:(c + 1) * tk, :].astype(F32).T.astype(BF16)

    lookahead = ATT_LOOKAHEAD
    lambda_init = lam_ref[0]
    lam = (jnp.exp(jnp.sum(lq1_ref[...] * lk1_ref[...], axis=-1, keepdims=True))
           - jnp.exp(jnp.sum(lq2_ref[...] * lk2_ref[...], axis=-1, keepdims=True))
           + lambda_init)

    def load_queries(i):
        qt = q_ref[pl.ds(pl.multiple_of(i * tq, tq), tq), :].astype(F32).T
        row = lax.broadcasted_iota(jnp.int32, qt.shape, 0)
        qpad_ref[...] = jnp.concatenate([jnp.where(row < DIFF_QK_DIM, qt, 0.0),
                                         jnp.where(row >= DIFF_QK_DIM, qt, 0.0)],
                                        axis=1).astype(BF16)

    def scores(c):
        s = jnp.dot(k_ref[c * tk:(c + 1) * tk, :], qpad_ref[...],
                    preferred_element_type=F32)
        s_ref[c % ATT_SCORE_BUFFERS] = s
        return jnp.max(s, axis=0, keepdims=True)

    def update(c, cmax, m, l):
        m_new = jnp.maximum(m, cmax)
        alpha = jnp.exp2(m - m_new)
        p = jnp.exp2(s_ref[c % ATT_SCORE_BUFFERS] - m_new)
        l = alpha * l + jnp.sum(p, axis=0, keepdims=True)
        pv = jnp.dot(vt_ref[c], p.astype(BF16), preferred_element_type=F32)
        acc_ref[...] = acc_ref[...] * alpha + pv
        return m_new, l

    def start_block(i):
        load_queries(i)
        return tuple(scores(c) for c in range(lookahead))

    def block(i, first_cmax):
        acc_ref[...] = jnp.zeros_like(acc_ref)
        m = jnp.full((1, 2 * tq), MASK_VALUE, F32)
        l = jnp.zeros((1, 2 * tq), F32)
        cmax = dict(enumerate(first_cmax))
        for c in range(n_chunks):
            if c + lookahead < n_chunks:
                cmax[c + lookahead] = scores(c + lookahead)
            m, l = update(c, cmax.pop(c), m, l)
        next_cmax = start_block(jnp.minimum(i + 1, n_blocks - 1))

        o = acc_ref[...] / l
        od = (o[:, :tq] - lam * o[:, tq:]).T
        y = _rmsnorm_rows(od, g_ref[...]) * (1.0 - lambda_init)
        o_ref[pl.ds(pl.multiple_of(i * tq, tq), tq), :] = y.astype(o_ref.dtype)
        return next_cmax

    lax.fori_loop(0, n_blocks, block, start_block(0))


def _diff_attention(proj, lam_init, lq1, lk1, lq2, lk2, subln_g):
    s = proj.shape[0]
    tq = min(ATT_TQ, s)
    tk = min(ATT_TK, s)
    n_chunks = s // tk
    assert n_chunks % ATT_SCORE_BUFFERS == 0, "score slots must line up across query blocks"
    assert ATT_SCORE_BUFFERS >= 2 * ATT_LOOKAHEAD
    kernel_fn = functools.partial(_diff_attn_kernel, tq=tq, tk=tk, n_chunks=n_chunks,
                                  n_blocks=s // tq)
    k_col0 = DIFF_W // LANES
    v_col0 = 2 * DIFF_W // LANES
    vec = lambda n: pl.BlockSpec((1, n), lambda h: (0, 0))
    return pl.pallas_call(
        kernel_fn,
        grid=(N_DIFF_HEADS,),
        in_specs=[pl.BlockSpec(memory_space=pltpu.SMEM),
                  pl.BlockSpec((s, LANES), lambda h: (0, h)),
                  pl.BlockSpec((s, LANES), lambda h: (0, k_col0 + h)),
                  pl.BlockSpec((s, LANES), lambda h: (0, v_col0 + h)),
                  vec(DIFF_QK_DIM), vec(DIFF_QK_DIM), vec(DIFF_QK_DIM), vec(DIFF_QK_DIM),
                  vec(DIFF_V_DIM)],
        out_specs=pl.BlockSpec((s, LANES), lambda h: (0, h)),
        out_shape=jax.ShapeDtypeStruct((s, DIFF_W), BF16),
        scratch_shapes=[pltpu.VMEM((n_chunks, DIFF_V_DIM, tk), BF16),
                        pltpu.VMEM((LANES, 2 * tq), BF16),
                        pltpu.VMEM((ATT_SCORE_BUFFERS, tk, 2 * tq), F32),
                        pltpu.VMEM((DIFF_V_DIM, 2 * tq), F32)],
        compiler_params=_compiler_params(("arbitrary",)),
        name="diff_attention",
    )(lam_init, proj, proj, proj,
      lq1.reshape(1, -1), lk1.reshape(1, -1), lq2.reshape(1, -1), lk2.reshape(1, -1),
      subln_g.reshape(1, -1))


NA_GROUP = 4
NA_SPAN = 3 * NA_GROUP


def _na_span_start(g, n_groups):
    return jnp.clip(g - 1, 0, n_groups - 3)


def _na_attn_kernel(q_ref, k0_ref, k1_ref, k2_ref, v0_ref, v1_ref, v2_ref, bias_ref, g_ref,
                    o_ref, kwin_ref, vwin_ref, o_scr):
    blk = NA_GROUP * GRID_W
    for t, (k_ref, v_ref) in enumerate(((k0_ref, v0_ref), (k1_ref, v1_ref), (k2_ref, v2_ref))):
        kwin_ref[t * blk:(t + 1) * blk, :] = k_ref[...]
        vwin_ref[t * blk:(t + 1) * blk, :] = v_ref[...]
    for h in range(N_NA_HEADS):
        cols = slice(h * NA_HEAD_DIM, (h + 1) * NA_HEAD_DIM)
        sc = lax.dot_general(q_ref[:, cols], kwin_ref[:, cols], (((1,), (1,)), ((), ())),
                             preferred_element_type=F32)
        sc = sc + bias_ref[h]
        p = jnp.exp2(sc - jnp.max(sc, axis=-1, keepdims=True))
        l = jnp.sum(p, axis=-1, keepdims=True)
        pv = jnp.dot(p.astype(BF16), vwin_ref[:, cols], preferred_element_type=F32)
        o_scr[:, cols] = pv / l
    o_ref[...] = _rmsnorm_rows(o_scr[...], g_ref[...]).astype(o_ref.dtype)


def _na_window_plan(rows):
    n_groups = rows // NA_GROUP
    kr = NA_WIN_ROWS

    def plan(g):
        start = NA_GROUP * int(np.clip(g - 1, 0, n_groups - 3))
        out = np.full((NA_GROUP, NA_SPAN), -1, np.int64)
        for j in range(NA_GROUP):
            r = NA_GROUP * g + j
            rs = int(np.clip(r - kr // 2, 0, rows - kr))
            for u in range(NA_SPAN):
                if rs <= start + u < rs + kr:
                    out[j, u] = start + u - r + NA_WIN_ROWS - 1
        assert (out >= 0).sum() == NA_GROUP * kr
        return out

    plans = [plan(0), plan(1), plan(n_groups - 1)]
    for g in range(1, n_groups - 1):
        assert (plan(g) == plans[1]).all()
    return plans


def _na_bias_table(rpb, rows):
    c = np.arange(GRID_W)
    cs = np.clip(c - NA_WIN_COLS // 2, 0, GRID_W - NA_WIN_COLS)
    kc = np.arange(GRID_W)
    col_valid = (kc[None, :] >= cs[:, None]) & (kc[None, :] < cs[:, None] + NA_WIN_COLS)
    pad = GRID_W - NA_WIN_COLS
    rp = jnp.pad(rpb.astype(F32) * LOG2_E, ((0, 0), (0, 0), (pad, pad)))
    by_col = jnp.stack([lax.slice_in_dim(rp, NA_WIN_COLS - 1 - ci + pad,
                                         NA_WIN_COLS - 1 - ci + pad + GRID_W, axis=2)
                        for ci in range(GRID_W)], axis=2)
    by_col = jnp.where(col_valid[None, None], by_col, MASK_VALUE)
    masked = jnp.full((rpb.shape[0], GRID_W, GRID_W), MASK_VALUE, F32)
    variants = []
    for plan in _na_window_plan(rows):
        rows_j = []
        for j in range(NA_GROUP):
            tiles = [by_col[:, plan[j, u]] if plan[j, u] >= 0 else masked
                     for u in range(NA_SPAN)]
            rows_j.append(jnp.concatenate(tiles, axis=-1))
        variants.append(jnp.concatenate(rows_j, axis=1))
    return jnp.stack(variants)


def _na_attention(proj, bias, layer, na_g):
    s = proj.shape[0]
    rows = s // GRID_W
    assert rows % NA_GROUP == 0 and rows >= NA_SPAN and NA_SPAN >= NA_GROUP + NA_WIN_ROWS - 1
    n_groups = rows // NA_GROUP
    blk = NA_GROUP * GRID_W
    q_col = 3 * DIFF_W // NA_W
    k_col, v_col = q_col + 1, q_col + 2

    def kv_spec(t, col):
        return pl.BlockSpec((blk, NA_W), lambda g: (_na_span_start(g, n_groups) + t, col))

    def bias_variant(g):
        return jnp.where(g == 0, 0, jnp.where(g == n_groups - 1, 2, 1))

    return pl.pallas_call(
        _na_attn_kernel,
        grid=(n_groups,),
        in_specs=[pl.BlockSpec((blk, NA_W), lambda g: (g, q_col))]
                 + [kv_spec(t, k_col) for t in range(3)]
                 + [kv_spec(t, v_col) for t in range(3)]
                 + [pl.BlockSpec((None, N_NA_HEADS, blk, NA_SPAN * GRID_W),
                                 lambda g: (bias_variant(g), layer, 0, 0)),
                    pl.BlockSpec((1, NA_W), lambda g: (0, 0))],
        out_specs=pl.BlockSpec((blk, NA_W), lambda g: (g, 0)),
        out_shape=jax.ShapeDtypeStruct((s, NA_W), BF16),
        scratch_shapes=[pltpu.VMEM((NA_SPAN * GRID_W, NA_W), BF16),
                        pltpu.VMEM((NA_SPAN * GRID_W, NA_W), BF16),
                        pltpu.VMEM((blk, NA_W), F32)],
        compiler_params=_compiler_params(("arbitrary",)),
        name="na_attention",
    )(*([proj] * 7), bias, na_g.reshape(1, -1))


def _out_proj_kernel(a_ref, b_ref, w_ref, x_ref, o_ref):
    ka = a_ref.shape[1]
    tr = a_ref.shape[0] // MM_ROW_SPLIT
    for r in range(MM_ROW_SPLIT):
        rows = slice(r * tr, (r + 1) * tr)
        acc = jnp.dot(a_ref[rows, :], w_ref[:ka, :], preferred_element_type=F32)
        acc += jnp.dot(b_ref[rows, :], w_ref[ka:, :], preferred_element_type=F32)
        o_ref[rows, :] = x_ref[rows, :] + acc


def _out_proj(o_diff, o_na, w, layer, x):
    s, d = x.shape
    tm = min(OUT_PROJ_TM, s)
    ka, kb = o_diff.shape[1], o_na.shape[1]
    return pl.pallas_call(
        _out_proj_kernel,
        grid=(s // tm,),
        in_specs=[pl.BlockSpec((tm, ka), lambda i: (i, 0)),
                  pl.BlockSpec((tm, kb), lambda i: (i, 0)),
                  pl.BlockSpec((None, ka + kb, d), lambda i: (layer, 0, 0)),
                  pl.BlockSpec((tm, d), lambda i: (i, 0))],
        out_specs=pl.BlockSpec((tm, d), lambda i: (i, 0)),
        out_shape=jax.ShapeDtypeStruct((s, d), F32),
        compiler_params=_compiler_params(("arbitrary",)),
        name="out_proj",
    )(o_diff, o_na, w, x)


def _mlp_out_kernel(u_ref, w_ref, x_ref, *rest, final_norm):
    o_ref = rest[-1]
    k = pl.program_id(1)

    @pl.when(k == 0)
    def _():
        o_ref[...] = x_ref[...]

    o_ref[...] += jnp.dot(u_ref[...], w_ref[...], preferred_element_type=F32)

    if final_norm:
        @pl.when(k == pl.num_programs(1) - 1)
        def _():
            o_ref[...] = _rmsnorm_rows(o_ref[...], rest[0][...])


def _mlp_out(u, w, layer, x, final_g=None):
    s, d = x.shape
    f = u.shape[1]
    tm = min(MM_TM, s)
    extra, extra_specs = (), []
    if final_g is not None:
        extra = (final_g.reshape(1, d),)
        extra_specs = [pl.BlockSpec((1, d), lambda i, k: (0, 0))]
    return pl.pallas_call(
        functools.partial(_mlp_out_kernel, final_norm=final_g is not None),
        grid=(s // tm, f // MLP_OUT_TK),
        in_specs=[pl.BlockSpec((tm, MLP_OUT_TK), lambda i, k: (i, k)),
                  pl.BlockSpec((None, MLP_OUT_TK, d), lambda i, k: (layer, k, 0)),
                  pl.BlockSpec((tm, d), lambda i, k: (i, 0))] + extra_specs,
        out_specs=pl.BlockSpec((tm, d), lambda i, k: (i, 0)),
        out_shape=jax.ShapeDtypeStruct((s, d), F32),
        compiler_params=_compiler_params(("arbitrary", "arbitrary")),
        name="mlp_out_final" if final_g is not None else "mlp_out",
    )(u, w, x, *extra)


def _rope_tables(seq):
    inv_freq = 1.0 / (ROPE_THETA ** (jnp.arange(0, DIFF_QK_DIM, 2, dtype=F32) / DIFF_QK_DIM))
    ang = jnp.arange(seq, dtype=F32)[:, None] * inv_freq[None, :]
    cos, sin = jnp.cos(ang), jnp.sin(ang)
    zero = jnp.zeros_like(sin)
    reps = LANES // DIFF_QK_DIM
    cos_t = jnp.tile(cos, (1, 2 * reps))
    sin_lo = jnp.tile(jnp.concatenate([-sin, zero], axis=1), (1, reps))
    sin_hi = jnp.tile(jnp.concatenate([zero, sin], axis=1), (1, reps))
    return cos_t, sin_lo, sin_hi


def kernel(x, attn_norm, w_in, lambda_q1, lambda_k1, lambda_q2, lambda_k2, diff_subln, na_norm,
           na_rpb, w_out, mlp_norm, w_mlp_in, w_mlp_out, final_norm):
    b, s, d = x.shape
    depth = w_in.shape[0]
    rope = _rope_tables(s)
    na_bias = _na_bias_table(na_rpb.reshape((depth * N_NA_HEADS,) + na_rpb.shape[2:]),
                             s // GRID_W)
    w_in, w_out, w_mlp_in, w_mlp_out = (w.astype(BF16) for w in (w_in, w_out, w_mlp_in, w_mlp_out))
    outs = []
    for bi in range(b):
        xb = x[bi].astype(F32)
        for l in range(depth):
            lambda_init = 0.8 - 0.6 * math.exp(-0.3 * l)
            proj = _norm_matmul(_in_proj_kernel, xb, attn_norm[l].astype(F32), w_in, l, MM_TN,
                                rope, name="in_proj")
            o_diff = _diff_attention(proj, jnp.full((1,), lambda_init, F32),
                                     lambda_q1[l].astype(F32), lambda_k1[l].astype(F32),
                                     lambda_q2[l].astype(F32), lambda_k2[l].astype(F32),
                                     diff_subln[l].astype(F32))
            o_na = _na_attention(proj, na_bias, l, na_norm[l].astype(F32))
            xb = _out_proj(o_diff, o_na, w_out, l, xb)
            u = _norm_matmul(_mlp_in_kernel, xb, mlp_norm[l].astype(F32), w_mlp_in, l,
                             MLP_IN_TN, name="mlp_in")
            xb = _mlp_out(u, w_mlp_out, l, xb,
                          final_norm.astype(F32) if l == depth - 1 else None)
        outs.append(xb)
    out = outs[0][None] if b == 1 else jnp.stack(outs)
    return out.astype(x.dtype)
```

```python
import functools
import math

import jax
import jax.numpy as jnp
import numpy as np
from jax import lax
from jax.experimental import pallas as pl
from jax.experimental.pallas import tpu as pltpu

F32 = jnp.float32
BF16 = jnp.bfloat16

GRID_W = 64
N_DIFF_HEADS = 8
DIFF_QK_DIM = 64
DIFF_V_DIM = 128
DIFF_W = N_DIFF_HEADS * DIFF_V_DIM
N_NA_HEADS = 8
NA_HEAD_DIM = 128
NA_W = N_NA_HEADS * NA_HEAD_DIM
NA_WIN_ROWS = 8
NA_WIN_COLS = 16
ROPE_THETA = 10000.0
EPS = 1e-6
MASK_VALUE = -1e30
LOG2_E = math.log2(math.e)

LANES = 128
VMEM_LIMIT_BYTES = 56 * 1024 * 1024

MM_TM = 1024
MM_TN = 1024
MLP_IN_TN = 2048
OUT_PROJ_TM = 512
MM_ROW_SPLIT = 4
MLP_OUT_TK = 1024
ATT_TQ = 256
ATT_TK = 512
ATT_LOOKAHEAD = 2
ATT_SCORE_BUFFERS = 4


def _compiler_params(semantics, flags=None):
    return pltpu.CompilerParams(dimension_semantics=semantics,
                                vmem_limit_bytes=VMEM_LIMIT_BYTES, flags=flags)


def _rmsnorm_rows(x, g):
    return x * lax.rsqrt(jnp.mean(x * x, axis=-1, keepdims=True) + EPS) * g


def _norm_rows_once(x_ref, g_ref, h_ref):
    @pl.when(pl.program_id(1) == 0)
    def _():
        h_ref[...] = _rmsnorm_rows(x_ref[...], g_ref[...]).astype(BF16)


def _matmul_row_blocks(h_ref, w_ref, epilogue):
    tr = h_ref.shape[0] // MM_ROW_SPLIT
    for r in range(MM_ROW_SPLIT):
        rows = slice(r * tr, (r + 1) * tr)
        epilogue(rows, jnp.dot(h_ref[rows, :], w_ref[...], preferred_element_type=F32))


def _in_proj_kernel(x_ref, g_ref, w_ref, cos_ref, sin_lo_ref, sin_hi_ref, o_ref, h_ref):
    j = pl.program_id(1)
    n_rope_blocks = 2 * DIFF_W // MM_TN
    n_q_blocks = DIFF_W // MM_TN
    _norm_rows_once(x_ref, g_ref, h_ref)

    @pl.when(j < n_rope_blocks)
    def _():
        scale = jnp.where(j < n_q_blocks, LOG2_E * DIFF_QK_DIM ** -0.5, 1.0).astype(F32)

        def rope(rows, acc):
            cos = cos_ref[rows, :] * scale
            sin_lo = sin_lo_ref[rows, :] * scale
            sin_hi = sin_hi_ref[rows, :] * scale
            for c in range(MM_TN // LANES):
                xs = acc[:, c * LANES:(c + 1) * LANES]
                r = (xs * cos
                     + pltpu.roll(xs, LANES - DIFF_QK_DIM // 2, 1) * sin_lo
                     + pltpu.roll(xs, DIFF_QK_DIM // 2, 1) * sin_hi)
                o_ref[rows, c * LANES:(c + 1) * LANES] = r.astype(o_ref.dtype)

        _matmul_row_blocks(h_ref, w_ref, rope)

    @pl.when(j >= n_rope_blocks)
    def _():
        is_na_q = (j >= 3 * DIFF_W // MM_TN) & (j < (3 * DIFF_W + NA_W) // MM_TN)
        scale = jnp.where(is_na_q, LOG2_E * NA_HEAD_DIM ** -0.5, 1.0).astype(F32)

        def scaled(rows, acc):
            o_ref[rows, :] = (acc * scale).astype(o_ref.dtype)

        _matmul_row_blocks(h_ref, w_ref, scaled)


def _mlp_in_kernel(x_ref, g_ref, w_ref, o_ref, h_ref):
    _norm_rows_once(x_ref, g_ref, h_ref)

    def relu2(rows, acc):
        u = jnp.maximum(acc, 0.0)
        o_ref[rows, :] = (u * u).astype(o_ref.dtype)

    _matmul_row_blocks(h_ref, w_ref, relu2)


def _norm_matmul(kernel_fn, x, g, w, layer, tn, extra=(), name=None):
    s, d = x.shape
    n = w.shape[2]
    tm = min(MM_TM, s)
    grid = (s // tm, n // tn)
    extra_specs = [pl.BlockSpec((tm, LANES), lambda i, j: (i, 0)) for _ in extra]
    return pl.pallas_call(
        kernel_fn,
        grid=grid,
        in_specs=[pl.BlockSpec((tm, d), lambda i, j: (i, 0)),
                  pl.BlockSpec((1, d), lambda i, j: (0, 0)),
                  pl.BlockSpec((None, d, tn), lambda i, j: (layer, 0, j))] + extra_specs,
        out_specs=pl.BlockSpec((tm, tn), lambda i, j: (i, j)),
        out_shape=jax.ShapeDtypeStruct((s, n), BF16),
        scratch_shapes=[pltpu.VMEM((tm, d), BF16)],
        compiler_params=_compiler_params(("arbitrary", "arbitrary")),
        name=name,
    )(x, g.reshape(1, d), w, *extra)


def _diff_attn_kernel(lam_ref, q_ref, k_ref, v_ref, lq1_ref, lk1_ref, lq2_ref, lk2_ref,
                      g_ref, o_ref, vt_ref, qpad_ref, s_ref, acc_ref,
                      *, tq, tk, n_chunks, n_blocks):
    for c in range(n_chunks):
        vt_ref[c] = v_ref[c * tk:(c + 1) * tk, :].astype(F32).T.astype(BF16)

    lookahead = ATT_LOOKAHEAD
    lambda_init = lam_ref[0]
    lam = (jnp.exp(jnp.sum(lq1_ref[...] * lk1_ref[...], axis=-1, keepdims=True))
           - jnp.exp(jnp.sum(lq2_ref[...] * lk2_ref[...], axis=-1, keepdims=True))
           + lambda_init)

    def load_queries(i):
        qt = q_ref[pl.ds(pl.multiple_of(i * tq, tq), tq), :].astype(F32).T
        row = lax.broadcasted_iota(jnp.int32, qt.shape, 0)
        qpad_ref[...] = jnp.concatenate([jnp.where(row < DIFF_QK_DIM, qt, 0.0),
                                         jnp.where(row >= DIFF_QK_DIM, qt, 0.0)],
                                        axis=1).astype(BF16)

    def scores(c):
        s = jnp.dot(k_ref[c * tk:(c + 1) * tk, :], qpad_ref[...],
                    preferred_element_type=F32)
        s_ref[c % ATT_SCORE_BUFFERS] = s
        return jnp.max(s, axis=0, keepdims=True)

    def update(c, cmax, m, l):
        m_new = jnp.maximum(m, cmax)
        alpha = jnp.exp2(m - m_new)
        p = jnp.exp2(s_ref[c % ATT_SCORE_BUFFERS] - m_new)
        l = alpha * l + jnp.sum(p, axis=0, keepdims=True)
        pv = jnp.dot(vt_ref[c], p.astype(BF16), preferred_element_type=F32)
        acc_ref[...] = acc_ref[...] * alpha + pv
        return m_new, l

    def start_block(i):
        load_queries(i)
        return tuple(scores(c) for c in range(lookahead))

    def block(i, first_cmax):
        acc_ref[...] = jnp.zeros_like(acc_ref)
        m = jnp.full((1, 2 * tq), MASK_VALUE, F32)
        l = jnp.zeros((1, 2 * tq), F32)
        cmax = dict(enumerate(first_cmax))
        for c in range(n_chunks):
            if c + lookahead < n_chunks:
                cmax[c + lookahead] = scores(c + lookahead)
            m, l = update(c, cmax.pop(c), m, l)
        next_cmax = start_block(jnp.minimum(i + 1, n_blocks - 1))

        o = acc_ref[...] / l
        od = (o[:, :tq] - lam * o[:, tq:]).T
        y = _rmsnorm_rows(od, g_ref[...]) * (1.0 - lambda_init)
        o_ref[pl.ds(pl.multiple_of(i * tq, tq), tq), :] = y.astype(o_ref.dtype)
        return next_cmax

    lax.fori_loop(0, n_blocks, block, start_block(0))


def _diff_attention(proj, lam_init, lq1, lk1, lq2, lk2, subln_g):
    s = proj.shape[0]
    tq = min(ATT_TQ, s)
    tk = min(ATT_TK, s)
    n_chunks = s // tk
    assert n_chunks % ATT_SCORE_BUFFERS == 0, "score slots must line up across query blocks"
    assert ATT_SCORE_BUFFERS >= 2 * ATT_LOOKAHEAD
    kernel_fn = functools.partial(_diff_attn_kernel, tq=tq, tk=tk, n_chunks=n_chunks,
                                  n_blocks=s // tq)
    k_col0 = DIFF_W // LANES
    v_col0 = 2 * DIFF_W // LANES
    vec = lambda n: pl.BlockSpec((1, n), lambda h: (0, 0))
    return pl.pallas_call(
        kernel_fn,
        grid=(N_DIFF_HEADS,),
        in_specs=[pl.BlockSpec(memory_space=pltpu.SMEM),
                  pl.BlockSpec((s, LANES), lambda h: (0, h)),
                  pl.BlockSpec((s, LANES), lambda h: (0, k_col0 + h)),
                  pl.BlockSpec((s, LANES), lambda h: (0, v_col0 + h)),
                  vec(DIFF_QK_DIM), vec(DIFF_QK_DIM), vec(DIFF_QK_DIM), vec(DIFF_QK_DIM),
                  vec(DIFF_V_DIM)],
        out_specs=pl.BlockSpec((s, LANES), lambda h: (0, h)),
        out_shape=jax.ShapeDtypeStruct((s, DIFF_W), BF16),
        scratch_shapes=[pltpu.VMEM((n_chunks, DIFF_V_DIM, tk), BF16),
                        pltpu.VMEM((LANES, 2 * tq), BF16),
                        pltpu.VMEM((ATT_SCORE_BUFFERS, tk, 2 * tq), F32),
                        pltpu.VMEM((DIFF_V_DIM, 2 * tq), F32)],
        compiler_params=_compiler_params(("arbitrary",)),
        name="diff_attention",
    )(lam_init, proj, proj, proj,
      lq1.reshape(1, -1), lk1.reshape(1, -1), lq2.reshape(1, -1), lk2.reshape(1, -1),
      subln_g.reshape(1, -1))


NA_GROUP = 4
NA_SPAN = 3 * NA_GROUP


def _na_span_start(g, n_groups):
    return jnp.clip(g - 1, 0, n_groups - 3)


def _na_attn_kernel(q_ref, k0_ref, k1_ref, k2_ref, v0_ref, v1_ref, v2_ref, bias_ref, g_ref,
                    o_ref, kwin_ref, vwin_ref, o_scr):
    blk = NA_GROUP * GRID_W
    for t, (k_ref, v_ref) in enumerate(((k0_ref, v0_ref), (k1_ref, v1_ref), (k2_ref, v2_ref))):
        kwin_ref[t * blk:(t + 1) * blk, :] = k_ref[...]
        vwin_ref[t * blk:(t + 1) * blk, :] = v_ref[...]
    for h in range(N_NA_HEADS):
        cols = slice(h * NA_HEAD_DIM, (h + 1) * NA_HEAD_DIM)
        sc = lax.dot_general(q_ref[:, cols], kwin_ref[:, cols], (((1,), (1,)), ((), ())),
                             preferred_element_type=F32)
        sc = sc + jnp.concatenate([bias_ref[j, h] for j in range(NA_GROUP)], axis=0)
        p = jnp.exp2(sc - jnp.max(sc, axis=-1, keepdims=True))
        l = jnp.sum(p, axis=-1, keepdims=True)
        pv = jnp.dot(p.astype(BF16), vwin_ref[:, cols], preferred_element_type=F32)
        o_scr[:, cols] = pv / l
    o_ref[...] = _rmsnorm_rows(o_scr[...], g_ref[...]).astype(o_ref.dtype)


def _na_window_plan(rows):
    n_groups = rows // NA_GROUP
    kr = NA_WIN_ROWS

    def plan(g):
        start = NA_GROUP * int(np.clip(g - 1, 0, n_groups - 3))
        out = np.full((NA_GROUP, NA_SPAN), -1, np.int64)
        for j in range(NA_GROUP):
            r = NA_GROUP * g + j
            rs = int(np.clip(r - kr // 2, 0, rows - kr))
            for u in range(NA_SPAN):
                if rs <= start + u < rs + kr:
                    out[j, u] = start + u - r + NA_WIN_ROWS - 1
        assert (out >= 0).sum() == NA_GROUP * kr
        return out

    plans = [plan(0), plan(1), plan(n_groups - 1)]
    for g in range(1, n_groups - 1):
        assert (plan(g) == plans[1]).all()
    return plans


def _na_bias_table(rpb, rows):
    c = np.arange(GRID_W)
    cs = np.clip(c - NA_WIN_COLS // 2, 0, GRID_W - NA_WIN_COLS)
    kc = np.arange(GRID_W)
    col_valid = (kc[None, :] >= cs[:, None]) & (kc[None, :] < cs[:, None] + NA_WIN_COLS)
    pad = GRID_W - NA_WIN_COLS
    rp = jnp.pad(rpb.astype(F32) * LOG2_E, ((0, 0), (0, 0), (pad, pad)))
    by_col = jnp.stack([lax.slice_in_dim(rp, NA_WIN_COLS - 1 - ci + pad,
                                         NA_WIN_COLS - 1 - ci + pad + GRID_W, axis=2)
                        for ci in range(GRID_W)], axis=1)
    by_col = jnp.where(col_valid[None, :, None, :], by_col, MASK_VALUE)
    plans = _na_window_plan(rows)
    starts = []
    for plan in plans:
        for j in range(NA_GROUP):
            valid_u = np.nonzero(plan[j] >= 0)[0]
            assert (np.diff(valid_u) == 1).all() and (np.diff(plan[j, valid_u]) == 1).all()
            starts.append(int(plan[j, valid_u[0]] - valid_u[0]))
    front = max(0, -min(starts))
    back = max(0, max(starts) + NA_SPAN - by_col.shape[2])
    padded = jnp.pad(by_col, ((0, 0), (0, 0), (front, back), (0, 0)))
    windows = []
    for plan, start in zip([p[j] for p in plans for j in range(NA_GROUP)], starts):
        win = lax.slice_in_dim(padded, start + front, start + front + NA_SPAN, axis=2)
        win = jnp.where((plan >= 0)[None, None, :, None], win, MASK_VALUE)
        windows.append(win.reshape(win.shape[0], GRID_W, NA_SPAN * GRID_W))
    return jnp.stack(windows).reshape(len(plans), NA_GROUP, rpb.shape[0], GRID_W,
                                      NA_SPAN * GRID_W)


def _na_attention(proj, bias, layer, na_g):
    s = proj.shape[0]
    rows = s // GRID_W
    assert rows % NA_GROUP == 0 and rows >= NA_SPAN and NA_SPAN >= NA_GROUP + NA_WIN_ROWS - 1
    n_groups = rows // NA_GROUP
    blk = NA_GROUP * GRID_W
    q_col = 3 * DIFF_W // NA_W
    k_col, v_col = q_col + 1, q_col + 2

    def kv_spec(t, col):
        return pl.BlockSpec((blk, NA_W), lambda g: (_na_span_start(g, n_groups) + t, col))

    def bias_variant(g):
        return jnp.where(g == 0, 0, jnp.where(g == n_groups - 1, 2, 1))

    return pl.pallas_call(
        _na_attn_kernel,
        grid=(n_groups,),
        in_specs=[pl.BlockSpec((blk, NA_W), lambda g: (g, q_col))]
                 + [kv_spec(t, k_col) for t in range(3)]
                 + [kv_spec(t, v_col) for t in range(3)]
                 + [pl.BlockSpec((None, NA_GROUP, N_NA_HEADS, GRID_W, NA_SPAN * GRID_W),
                                 lambda g: (bias_variant(g), 0, layer, 0, 0)),
                    pl.BlockSpec((1, NA_W), lambda g: (0, 0))],
        out_specs=pl.BlockSpec((blk, NA_W), lambda g: (g, 0)),
        out_shape=jax.ShapeDtypeStruct((s, NA_W), BF16),
        scratch_shapes=[pltpu.VMEM((NA_SPAN * GRID_W, NA_W), BF16),
                        pltpu.VMEM((NA_SPAN * GRID_W, NA_W), BF16),
                        pltpu.VMEM((blk, NA_W), F32)],
        compiler_params=_compiler_params(("arbitrary",)),
        name="na_attention",
    )(*([proj] * 7), bias, na_g.reshape(1, -1))


def _out_proj_kernel(a_ref, b_ref, w_ref, x_ref, o_ref):
    ka = a_ref.shape[1]
    tr = a_ref.shape[0] // MM_ROW_SPLIT
    for r in range(MM_ROW_SPLIT):
        rows = slice(r * tr, (r + 1) * tr)
        acc = jnp.dot(a_ref[rows, :], w_ref[:ka, :], preferred_element_type=F32)
        acc += jnp.dot(b_ref[rows, :], w_ref[ka:, :], preferred_element_type=F32)
        o_ref[rows, :] = x_ref[rows, :] + acc


def _out_proj(o_diff, o_na, w, layer, x):
    s, d = x.shape
    tm = min(OUT_PROJ_TM, s)
    ka, kb = o_diff.shape[1], o_na.shape[1]
    return pl.pallas_call(
        _out_proj_kernel,
        grid=(s // tm,),
        in_specs=[pl.BlockSpec((tm, ka), lambda i: (i, 0)),
                  pl.BlockSpec((tm, kb), lambda i: (i, 0)),
                  pl.BlockSpec((None, ka + kb, d), lambda i: (layer, 0, 0)),
                  pl.BlockSpec((tm, d), lambda i: (i, 0))],
        out_specs=pl.BlockSpec((tm, d), lambda i: (i, 0)),
        out_shape=jax.ShapeDtypeStruct((s, d), F32),
        compiler_params=_compiler_params(("arbitrary",)),
        name="out_proj",
    )(o_diff, o_na, w, x)


def _mlp_out_kernel(u_ref, w_ref, x_ref, *rest, final_norm):
    o_ref = rest[-1]
    k = pl.program_id(1)

    @pl.when(k == 0)
    def _():
        o_ref[...] = x_ref[...]

    o_ref[...] += jnp.dot(u_ref[...], w_ref[...], preferred_element_type=F32)

    if final_norm:
        @pl.when(k == pl.num_programs(1) - 1)
        def _():
            o_ref[...] = _rmsnorm_rows(o_ref[...], rest[0][...])


def _mlp_out(u, w, layer, x, final_g=None):
    s, d = x.shape
    f = u.shape[1]
    tm = min(MM_TM, s)
    extra, extra_specs = (), []
    if final_g is not None:
        extra = (final_g.reshape(1, d),)
        extra_specs = [pl.BlockSpec((1, d), lambda i, k: (0, 0))]
    return pl.pallas_call(
        functools.partial(_mlp_out_kernel, final_norm=final_g is not None),
        grid=(s // tm, f // MLP_OUT_TK),
        in_specs=[pl.BlockSpec((tm, MLP_OUT_TK), lambda i, k: (i, k)),
                  pl.BlockSpec((None, MLP_OUT_TK, d), lambda i, k: (layer, k, 0)),
                  pl.BlockSpec((tm, d), lambda i, k: (i, 0))] + extra_specs,
        out_specs=pl.BlockSpec((tm, d), lambda i, k: (i, 0)),
        out_shape=jax.ShapeDtypeStruct((s, d), F32),
        compiler_params=_compiler_params(("arbitrary", "arbitrary")),
        name="mlp_out_final" if final_g is not None else "mlp_out",
    )(u, w, x, *extra)


def _rope_tables(seq):
    inv_freq = 1.0 / (ROPE_THETA ** (jnp.arange(0, DIFF_QK_DIM, 2, dtype=F32) / DIFF_QK_DIM))
    ang = jnp.arange(seq, dtype=F32)[:, None] * inv_freq[None, :]
    cos, sin = jnp.cos(ang), jnp.sin(ang)
    zero = jnp.zeros_like(sin)
    reps = LANES // DIFF_QK_DIM
    cos_t = jnp.tile(cos, (1, 2 * reps))
    sin_lo = jnp.tile(jnp.concatenate([-sin, zero], axis=1), (1, reps))
    sin_hi = jnp.tile(jnp.concatenate([zero, sin], axis=1), (1, reps))
    return cos_t, sin_lo, sin_hi


def kernel(x, attn_norm, w_in, lambda_q1, lambda_k1, lambda_q2, lambda_k2, diff_subln, na_norm,
           na_rpb, w_out, mlp_norm, w_mlp_in, w_mlp_out, final_norm):
    b, s, d = x.shape
    depth = w_in.shape[0]
    rope = _rope_tables(s)
    na_bias = _na_bias_table(na_rpb.reshape((depth * N_NA_HEADS,) + na_rpb.shape[2:]),
                             s // GRID_W)
    w_in, w_out, w_mlp_in, w_mlp_out = (w.astype(BF16) for w in (w_in, w_out, w_mlp_in, w_mlp_out))
    outs = []
    for bi in range(b):
        xb = x[bi].astype(F32)
        for l in range(depth):
            lambda_init = 0.8 - 0.6 * math.exp(-0.3 * l)
            proj = _norm_matmul(_in_proj_kernel, xb, attn_norm[l].astype(F32), w_in, l, MM_TN,
                                rope, name="in_proj")
            o_diff = _diff_attention(proj, jnp.full((1,), lambda_init, F32),
                                     lambda_q1[l].astype(F32), lambda_k1[l].astype(F32),
                                     lambda_q2[l].astype(F32), lambda_k2[l].astype(F32),
                                     diff_subln[l].astype(F32))
            o_na = _na_attention(proj, na_bias, l, na_norm[l].astype(F32))
            xb = _out_proj(o_diff, o_na, w_out, l, xb)
            u = _norm_matmul(_mlp_in_kernel, xb, mlp_norm[l].astype(F32), w_mlp_in, l,
                             MLP_IN_TN, name="mlp_in")
            xb = _mlp_out(u, w_mlp_out, l, xb,
                          final_norm.astype(F32) if l == depth - 1 else None)
        outs.append(xb)
    out = outs[0][None] if b == 1 else jnp.stack(outs)
    return out.astype(x.dtype)
```

```python
import functools
import math

import jax
import jax.numpy as jnp
import numpy as np
from jax import lax
from jax.experimental import pallas as pl
from jax.experimental.pallas import tpu as pltpu

F32 = jnp.float32
BF16 = jnp.bfloat16

GRID_W = 64
N_DIFF_HEADS = 8
DIFF_QK_DIM = 64
DIFF_V_DIM = 128
DIFF_W = N_DIFF_HEADS * DIFF_V_DIM
N_NA_HEADS = 8
NA_HEAD_DIM = 128
NA_W = N_NA_HEADS * NA_HEAD_DIM
NA_WIN_ROWS = 8
NA_WIN_COLS = 16
ROPE_THETA = 10000.0
EPS = 1e-6
MASK_VALUE = -1e30
LOG2_E = math.log2(math.e)

LANES = 128
VMEM_LIMIT_BYTES = 56 * 1024 * 1024

MM_TM = 1024
MM_TN = 1024
MLP_IN_TN = 2048
OUT_PROJ_TM = 512
MM_ROW_SPLIT = 4
MLP_OUT_TK = 1024
ATT_TQ = 256
ATT_TK = 512
ATT_LOOKAHEAD = 2
ATT_SCORE_BUFFERS = 4


def _compiler_params(semantics, flags=None):
    return pltpu.CompilerParams(dimension_semantics=semantics,
                                vmem_limit_bytes=VMEM_LIMIT_BYTES, flags=flags)


def _rmsnorm_rows(x, g):
    return x * lax.rsqrt(jnp.mean(x * x, axis=-1, keepdims=True) + EPS) * g


def _norm_rows_once(x_ref, g_ref, h_ref):
    @pl.when(pl.program_id(1) == 0)
    def _():
        h_ref[...] = _rmsnorm_rows(x_ref[...], g_ref[...]).astype(BF16)


def _matmul_row_blocks(h_ref, w_ref, epilogue):
    tr = h_ref.shape[0] // MM_ROW_SPLIT
    for r in range(MM_ROW_SPLIT):
        rows = slice(r * tr, (r + 1) * tr)
        epilogue(rows, jnp.dot(h_ref[rows, :], w_ref[...], preferred_element_type=F32))


def _in_proj_kernel(x_ref, g_ref, w_ref, cos_ref, sin_lo_ref, sin_hi_ref, o_ref, h_ref):
    j = pl.program_id(1)
    n_rope_blocks = 2 * DIFF_W // MM_TN
    n_q_blocks = DIFF_W // MM_TN
    _norm_rows_once(x_ref, g_ref, h_ref)

    @pl.when(j < n_rope_blocks)
    def _():
        scale = jnp.where(j < n_q_blocks, LOG2_E * DIFF_QK_DIM ** -0.5, 1.0).astype(F32)

        def rope(rows, acc):
            cos = cos_ref[rows, :] * scale
            sin_lo = sin_lo_ref[rows, :] * scale
            sin_hi = sin_hi_ref[rows, :] * scale
            for c in range(MM_TN // LANES):
                xs = acc[:, c * LANES:(c + 1) * LANES]
                r = (xs * cos
                     + pltpu.roll(xs, LANES - DIFF_QK_DIM // 2, 1) * sin_lo
                     + pltpu.roll(xs, DIFF_QK_DIM // 2, 1) * sin_hi)
                o_ref[rows, c * LANES:(c + 1) * LANES] = r.astype(o_ref.dtype)

        _matmul_row_blocks(h_ref, w_ref, rope)

    @pl.when(j >= n_rope_blocks)
    def _():
        is_na_q = (j >= 3 * DIFF_W // MM_TN) & (j < (3 * DIFF_W + NA_W) // MM_TN)
        scale = jnp.where(is_na_q, LOG2_E * NA_HEAD_DIM ** -0.5, 1.0).astype(F32)

        def scaled(rows, acc):
            o_ref[rows, :] = (acc * scale).astype(o_ref.dtype)

        _matmul_row_blocks(h_ref, w_ref, scaled)


def _mlp_in_kernel(x_ref, g_ref, w_ref, o_ref, h_ref):
    _norm_rows_once(x_ref, g_ref, h_ref)

    def relu2(rows, acc):
        u = jnp.maximum(acc, 0.0)
        o_ref[rows, :] = (u * u).astype(o_ref.dtype)

    _matmul_row_blocks(h_ref, w_ref, relu2)


def _norm_matmul(kernel_fn, x, g, w, layer, tn, extra=(), name=None):
    s, d = x.shape
    n = w.shape[2]
    tm = min(MM_TM, s)
    grid = (s // tm, n // tn)
    extra_specs = [pl.BlockSpec((tm, LANES), lambda i, j: (i, 0)) for _ in extra]
    return pl.pallas_call(
        kernel_fn,
        grid=grid,
        in_specs=[pl.BlockSpec((tm, d), lambda i, j: (i, 0)),
                  pl.BlockSpec((1, d), lambda i, j: (0, 0)),
                  pl.BlockSpec((None, d, tn), lambda i, j: (layer, 0, j))] + extra_specs,
        out_specs=pl.BlockSpec((tm, tn), lambda i, j: (i, j)),
        out_shape=jax.ShapeDtypeStruct((s, n), BF16),
        scratch_shapes=[pltpu.VMEM((tm, d), BF16)],
        compiler_params=_compiler_params(("arbitrary", "arbitrary")),
        name=name,
    )(x, g.reshape(1, d), w, *extra)


def _diff_attn_kernel(lam_ref, q_ref, k_ref, v_ref, lq1_ref, lk1_ref, lq2_ref, lk2_ref,
                      g_ref, o_ref, vt_ref, qpad_ref, s_ref, acc_ref,
                      *, tq, tk, n_chunks, n_blocks):
    for c in range(n_chunks):
        vt_ref[c] = v_ref[c * tk:(c + 1) * tk, :].astype(F32).T.astype(BF16)

    lookahead = ATT_LOOKAHEAD
    lambda_init = lam_ref[0]
    lam = (jnp.exp(jnp.sum(lq1_ref[...] * lk1_ref[...], axis=-1, keepdims=True))
           - jnp.exp(jnp.sum(lq2_ref[...] * lk2_ref[...], axis=-1, keepdims=True))
           + lambda_init)

    def load_queries(i):
        qt = q_ref[pl.ds(pl.multiple_of(i * tq, tq), tq), :].astype(F32).T
        row = lax.broadcasted_iota(jnp.int32, qt.shape, 0)
        qpad_ref[...] = jnp.concatenate([jnp.where(row < DIFF_QK_DIM, qt, 0.0),
                                         jnp.where(row >= DIFF_QK_DIM, qt, 0.0)],
                                        axis=1).astype(BF16)

    def scores(c):
        s = jnp.dot(k_ref[c * tk:(c + 1) * tk, :], qpad_ref[...],
                    preferred_element_type=F32)
        s_ref[c % ATT_SCORE_BUFFERS] = s
        return jnp.max(s, axis=0, keepdims=True)

    def update(c, cmax, m, l):
        m_new = jnp.maximum(m, cmax)
        alpha = jnp.exp2(m - m_new)
        p = jnp.exp2(s_ref[c % ATT_SCORE_BUFFERS] - m_new)
        l = alpha * l + jnp.sum(p, axis=0, keepdims=True)
        pv = jnp.dot(vt_ref[c], p.astype(BF16), preferred_element_type=F32)
        acc_ref[...] = acc_ref[...] * alpha + pv
        return m_new, l

    def start_block(i):
        load_queries(i)
        return tuple(scores(c) for c in range(lookahead))

    def block(i, first_cmax):
        acc_ref[...] = jnp.zeros_like(acc_ref)
        m = jnp.full((1, 2 * tq), MASK_VALUE, F32)
        l = jnp.zeros((1, 2 * tq), F32)
        cmax = dict(enumerate(first_cmax))
        for c in range(n_chunks):
            if c + lookahead < n_chunks:
                cmax[c + lookahead] = scores(c + lookahead)
            m, l = update(c, cmax.pop(c), m, l)
        next_cmax = start_block(jnp.minimum(i + 1, n_blocks - 1))

        o = acc_ref[...] / l
        od = (o[:, :tq] - lam * o[:, tq:]).T
        y = _rmsnorm_rows(od, g_ref[...]) * (1.0 - lambda_init)
        o_ref[pl.ds(pl.multiple_of(i * tq, tq), tq), :] = y.astype(o_ref.dtype)
        return next_cmax

    lax.fori_loop(0, n_blocks, block, start_block(0))


def _diff_attention(proj, lam_init, lq1, lk1, lq2, lk2, subln_g):
    s = proj.shape[0]
    tq = min(ATT_TQ, s)
    tk = min(ATT_TK, s)
    n_chunks = s // tk
    assert n_chunks % ATT_SCORE_BUFFERS == 0, "score slots must line up across query blocks"
    assert ATT_SCORE_BUFFERS >= 2 * ATT_LOOKAHEAD
    kernel_fn = functools.partial(_diff_attn_kernel, tq=tq, tk=tk, n_chunks=n_chunks,
                                  n_blocks=s // tq)
    k_col0 = DIFF_W // LANES
    v_col0 = 2 * DIFF_W // LANES
    vec = lambda n: pl.BlockSpec((1, n), lambda h: (0, 0))
    return pl.pallas_call(
        kernel_fn,
        grid=(N_DIFF_HEADS,),
        in_specs=[pl.BlockSpec(memory_space=pltpu.SMEM),
                  pl.BlockSpec((s, LANES), lambda h: (0, h)),
                  pl.BlockSpec((s, LANES), lambda h: (0, k_col0 + h)),
                  pl.BlockSpec((s, LANES), lambda h: (0, v_col0 + h)),
                  vec(DIFF_QK_DIM), vec(DIFF_QK_DIM), vec(DIFF_QK_DIM), vec(DIFF_QK_DIM),
                  vec(DIFF_V_DIM)],
        out_specs=pl.BlockSpec((s, LANES), lambda h: (0, h)),
        out_shape=jax.ShapeDtypeStruct((s, DIFF_W), BF16),
        scratch_shapes=[pltpu.VMEM((n_chunks, DIFF_V_DIM, tk), BF16),
                        pltpu.VMEM((LANES, 2 * tq), BF16),
                        pltpu.VMEM((ATT_SCORE_BUFFERS, tk, 2 * tq), F32),
                        pltpu.VMEM((DIFF_V_DIM, 2 * tq), F32)],
        compiler_params=_compiler_params(("arbitrary",)),
        name="diff_attention",
    )(lam_init, proj, proj, proj,
      lq1.reshape(1, -1), lk1.reshape(1, -1), lq2.reshape(1, -1), lk2.reshape(1, -1),
      subln_g.reshape(1, -1))


NA_GROUP = 4
NA_SPAN = 3 * NA_GROUP


def _na_span_start(g, n_groups):
    return jnp.clip(g - 1, 0, n_groups - 3)


def _na_attn_kernel(plan_ref, q_ref, k0_ref, k1_ref, k2_ref, v0_ref, v1_ref, v2_ref, bias_ref,
                    g_ref, o_ref, kwin_ref, vwin_ref, o_scr, *, n_groups):
    blk = NA_GROUP * GRID_W
    for t, (k_ref, v_ref) in enumerate(((k0_ref, v0_ref), (k1_ref, v1_ref), (k2_ref, v2_ref))):
        kwin_ref[t * blk:(t + 1) * blk, :] = k_ref[...]
        vwin_ref[t * blk:(t + 1) * blk, :] = v_ref[...]

    g = pl.program_id(0)
    variant = jnp.where(g == 0, 0, jnp.where(g == n_groups - 1, 2, 1))
    pairs = NA_SPAN // 2
    lane_half = lax.broadcasted_iota(jnp.int32, (1, 2 * GRID_W), 1) // GRID_W
    starts, row_masks = [], []
    for j in range(NA_GROUP):
        base = (variant * NA_GROUP + j) * 3
        starts.append(plan_ref[base])
        lo, hi = plan_ref[base + 1], plan_ref[base + 2]
        u = [2 * t + lane_half for t in range(pairs)]
        row_masks.append([(ut >= lo) & (ut < hi) for ut in u])

    def bias_rows(h, j):
        tiles = [jnp.where(row_masks[j][t], bias_ref[h, starts[j] + 2 * t], MASK_VALUE)
                 for t in range(pairs)]
        return jnp.concatenate(tiles, axis=1)

    for h in range(N_NA_HEADS):
        cols = slice(h * NA_HEAD_DIM, (h + 1) * NA_HEAD_DIM)
        sc = lax.dot_general(q_ref[:, cols], kwin_ref[:, cols], (((1,), (1,)), ((), ())),
                             preferred_element_type=F32)
        sc = sc + jnp.concatenate([bias_rows(h, j) for j in range(NA_GROUP)], axis=0)
        p = jnp.exp2(sc - jnp.max(sc, axis=-1, keepdims=True))
        l = jnp.sum(p, axis=-1, keepdims=True)
        pv = jnp.dot(p.astype(BF16), vwin_ref[:, cols], preferred_element_type=F32)
        o_scr[:, cols] = pv / l
    o_ref[...] = _rmsnorm_rows(o_scr[...], g_ref[...]).astype(o_ref.dtype)


def _na_window_plan(rows):
    n_groups = rows // NA_GROUP
    kr = NA_WIN_ROWS

    def plan(g):
        start = NA_GROUP * int(np.clip(g - 1, 0, n_groups - 3))
        out = np.full((NA_GROUP, NA_SPAN), -1, np.int64)
        for j in range(NA_GROUP):
            r = NA_GROUP * g + j
            rs = int(np.clip(r - kr // 2, 0, rows - kr))
            for u in range(NA_SPAN):
                if rs <= start + u < rs + kr:
                    out[j, u] = start + u - r + NA_WIN_ROWS - 1
        assert (out >= 0).sum() == NA_GROUP * kr
        return out

    plans = [plan(0), plan(1), plan(n_groups - 1)]
    for g in range(1, n_groups - 1):
        assert (plan(g) == plans[1]).all()
    return plans


def _na_row_windows(rows):
    out = []
    for plan in _na_window_plan(rows):
        for j in range(NA_GROUP):
            valid_u = np.nonzero(plan[j] >= 0)[0]
            assert (np.diff(valid_u) == 1).all() and (np.diff(plan[j, valid_u]) == 1).all()
            out.append((int(plan[j, valid_u[0]] - valid_u[0]), int(valid_u[0]),
                        int(valid_u[-1]) + 1))
    return out


def _na_bias_table(rpb, rows):
    c = np.arange(GRID_W)
    cs = np.clip(c - NA_WIN_COLS // 2, 0, GRID_W - NA_WIN_COLS)
    kc = np.arange(GRID_W)
    col_valid = (kc[None, :] >= cs[:, None]) & (kc[None, :] < cs[:, None] + NA_WIN_COLS)
    pad = GRID_W - NA_WIN_COLS
    rp = jnp.pad(rpb.astype(F32) * LOG2_E, ((0, 0), (0, 0), (pad, pad)))
    by_col = jnp.stack([lax.slice_in_dim(rp, NA_WIN_COLS - 1 - ci + pad,
                                         NA_WIN_COLS - 1 - ci + pad + GRID_W, axis=2)
                        for ci in range(GRID_W)], axis=2)
    by_col = jnp.where(col_valid[None, None], by_col, MASK_VALUE)
    windows = _na_row_windows(rows)
    front = max(0, -min(w[0] for w in windows))
    back = max(0, max(w[0] for w in windows) + NA_SPAN - by_col.shape[1])
    padded = jnp.pad(by_col, ((0, 0), (front, back + 1), (0, 0), (0, 0)))
    table = jnp.concatenate([padded[:, :-1], padded[:, 1:]], axis=-1)
    plan = np.array([[w[0] + front, w[1], w[2]] for w in windows], np.int32).reshape(-1)
    return table, jnp.asarray(plan)


def _na_attention(proj, bias, plan, layer, na_g):
    s = proj.shape[0]
    rows = s // GRID_W
    assert rows % NA_GROUP == 0 and rows >= NA_SPAN and NA_SPAN >= NA_GROUP + NA_WIN_ROWS - 1
    assert NA_SPAN % 2 == 0
    n_groups = rows // NA_GROUP
    blk = NA_GROUP * GRID_W
    q_col = 3 * DIFF_W // NA_W
    k_col, v_col = q_col + 1, q_col + 2
    bias = bias.reshape((-1, N_NA_HEADS) + bias.shape[1:])

    def kv_spec(t, col):
        return pl.BlockSpec((blk, NA_W),
                            lambda g, plan_ref: (_na_span_start(g, n_groups) + t, col))

    grid_spec = pltpu.PrefetchScalarGridSpec(
        num_scalar_prefetch=1,
        grid=(n_groups,),
        in_specs=[pl.BlockSpec((blk, NA_W), lambda g, plan_ref: (g, q_col))]
                 + [kv_spec(t, k_col) for t in range(3)]
                 + [kv_spec(t, v_col) for t in range(3)]
                 + [pl.BlockSpec((None,) + bias.shape[1:],
                                 lambda g, plan_ref: (layer, 0, 0, 0, 0)),
                    pl.BlockSpec((1, NA_W), lambda g, plan_ref: (0, 0))],
        out_specs=pl.BlockSpec((blk, NA_W), lambda g, plan_ref: (g, 0)),
        scratch_shapes=[pltpu.VMEM((NA_SPAN * GRID_W, NA_W), BF16),
                        pltpu.VMEM((NA_SPAN * GRID_W, NA_W), BF16),
                        pltpu.VMEM((blk, NA_W), F32)])
    return pl.pallas_call(
        functools.partial(_na_attn_kernel, n_groups=n_groups),
        grid_spec=grid_spec,
        out_shape=jax.ShapeDtypeStruct((s, NA_W), BF16),
        compiler_params=_compiler_params(("arbitrary",)),
        name="na_attention",
    )(plan, *([proj] * 7), bias, na_g.reshape(1, -1))


def _out_proj_kernel(a_ref, b_ref, w_ref, x_ref, o_ref):
    ka = a_ref.shape[1]
    tr = a_ref.shape[0] // MM_ROW_SPLIT
    for r in range(MM_ROW_SPLIT):
        rows = slice(r * tr, (r + 1) * tr)
        acc = jnp.dot(a_ref[rows, :], w_ref[:ka, :], preferred_element_type=F32)
        acc += jnp.dot(b_ref[rows, :], w_ref[ka:, :], preferred_element_type=F32)
        o_ref[rows, :] = x_ref[rows, :] + acc


def _out_proj(o_diff, o_na, w, layer, x):
    s, d = x.shape
    tm = min(OUT_PROJ_TM, s)
    ka, kb = o_diff.shape[1], o_na.shape[1]
    return pl.pallas_call(
        _out_proj_kernel,
        grid=(s // tm,),
        in_specs=[pl.BlockSpec((tm, ka), lambda i: (i, 0)),
                  pl.BlockSpec((tm, kb), lambda i: (i, 0)),
                  pl.BlockSpec((None, ka + kb, d), lambda i: (layer, 0, 0)),
                  pl.BlockSpec((tm, d), lambda i: (i, 0))],
        out_specs=pl.BlockSpec((tm, d), lambda i: (i, 0)),
        out_shape=jax.ShapeDtypeStruct((s, d), F32),
        compiler_params=_compiler_params(("arbitrary",)),
        name="out_proj",
    )(o_diff, o_na, w, x)


def _mlp_out_kernel(u_ref, w_ref, x_ref, *rest, final_norm):
    o_ref = rest[-1]
    k = pl.program_id(1)

    @pl.when(k == 0)
    def _():
        o_ref[...] = x_ref[...]

    o_ref[...] += jnp.dot(u_ref[...], w_ref[...], preferred_element_type=F32)

    if final_norm:
        @pl.when(k == pl.num_programs(1) - 1)
        def _():
            o_ref[...] = _rmsnorm_rows(o_ref[...], rest[0][...])


def _mlp_out(u, w, layer, x, final_g=None):
    s, d = x.shape
    f = u.shape[1]
    tm = min(MM_TM, s)
    extra, extra_specs = (), []
    if final_g is not None:
        extra = (final_g.reshape(1, d),)
        extra_specs = [pl.BlockSpec((1, d), lambda i, k: (0, 0))]
    return pl.pallas_call(
        functools.partial(_mlp_out_kernel, final_norm=final_g is not None),
        grid=(s // tm, f // MLP_OUT_TK),
        in_specs=[pl.BlockSpec((tm, MLP_OUT_TK), lambda i, k: (i, k)),
                  pl.BlockSpec((None, MLP_OUT_TK, d), lambda i, k: (layer, k, 0)),
                  pl.BlockSpec((tm, d), lambda i, k: (i, 0))] + extra_specs,
        out_specs=pl.BlockSpec((tm, d), lambda i, k: (i, 0)),
        out_shape=jax.ShapeDtypeStruct((s, d), F32),
        compiler_params=_compiler_params(("arbitrary", "arbitrary")),
        name="mlp_out_final" if final_g is not None else "mlp_out",
    )(u, w, x, *extra)


def _rope_tables(seq):
    inv_freq = 1.0 / (ROPE_THETA ** (jnp.arange(0, DIFF_QK_DIM, 2, dtype=F32) / DIFF_QK_DIM))
    ang = jnp.arange(seq, dtype=F32)[:, None] * inv_freq[None, :]
    cos, sin = jnp.cos(ang), jnp.sin(ang)
    zero = jnp.zeros_like(sin)
    reps = LANES // DIFF_QK_DIM
    cos_t = jnp.tile(cos, (1, 2 * reps))
    sin_lo = jnp.tile(jnp.concatenate([-sin, zero], axis=1), (1, reps))
    sin_hi = jnp.tile(jnp.concatenate([zero, sin], axis=1), (1, reps))
    return cos_t, sin_lo, sin_hi


def kernel(x, attn_norm, w_in, lambda_q1, lambda_k1, lambda_q2, lambda_k2, diff_subln, na_norm,
           na_rpb, w_out, mlp_norm, w_mlp_in, w_mlp_out, final_norm):
    b, s, d = x.shape
    depth = w_in.shape[0]
    rope = _rope_tables(s)
    na_bias, na_plan = _na_bias_table(
        na_rpb.reshape((depth * N_NA_HEADS,) + na_rpb.shape[2:]), s // GRID_W)
    w_in, w_out, w_mlp_in, w_mlp_out = (w.astype(BF16) for w in (w_in, w_out, w_mlp_in, w_mlp_out))
    outs = []
    for bi in range(b):
        xb = x[bi].astype(F32)
        for l in range(depth):
            lambda_init = 0.8 - 0.6 * math.exp(-0.3 * l)
            proj = _norm_matmul(_in_proj_kernel, xb, attn_norm[l].astype(F32), w_in, l, MM_TN,
                                rope, name="in_proj")
            o_diff = _diff_attention(proj, jnp.full((1,), lambda_init, F32),
                                     lambda_q1[l].astype(F32), lambda_k1[l].astype(F32),
                                     lambda_q2[l].astype(F32), lambda_k2[l].astype(F32),
                                     diff_subln[l].astype(F32))
            o_na = _na_attention(proj, na_bias, na_plan, l, na_norm[l].astype(F32))
            xb = _out_proj(o_diff, o_na, w_out, l, xb)
            u = _norm_matmul(_mlp_in_kernel, xb, mlp_norm[l].astype(F32), w_mlp_in, l,
                             MLP_IN_TN, name="mlp_in")
            xb = _mlp_out(u, w_mlp_out, l, xb,
                          final_norm.astype(F32) if l == depth - 1 else None)
        outs.append(xb)
    out = outs[0][None] if b == 1 else jnp.stack(outs)
    return out.astype(x.dtype)
```

```python
import functools
import math

import jax
import jax.numpy as jnp
import numpy as np
from jax import lax
from jax.experimental import pallas as pl
from jax.experimental.pallas import tpu as pltpu

F32 = jnp.float32
BF16 = jnp.bfloat16

GRID_W = 64
N_DIFF_HEADS = 8
DIFF_QK_DIM = 64
DIFF_V_DIM = 128
DIFF_W = N_DIFF_HEADS * DIFF_V_DIM
N_NA_HEADS = 8
NA_HEAD_DIM = 128
NA_W = N_NA_HEADS * NA_HEAD_DIM
NA_WIN_ROWS = 8
NA_WIN_COLS = 16
ROPE_THETA = 10000.0
EPS = 1e-6
MASK_VALUE = -1e30
LOG2_E = math.log2(math.e)

LANES = 128
VMEM_LIMIT_BYTES = 56 * 1024 * 1024

MM_TM = 1024
MM_TN = 1024
MLP_IN_TM = 2048
MLP_IN_TN = 1024
OUT_PROJ_TM = 512
MM_ROW_SPLIT = 4
MLP_OUT_TK = 1024
ATT_TQ = 256
ATT_TK = 512
ATT_LOOKAHEAD = 2
ATT_SCORE_BUFFERS = 4


def _compiler_params(semantics, flags=None):
    return pltpu.CompilerParams(dimension_semantics=semantics,
                                vmem_limit_bytes=VMEM_LIMIT_BYTES, flags=flags)


def _rmsnorm_rows(x, g):
    return x * lax.rsqrt(jnp.mean(x * x, axis=-1, keepdims=True) + EPS) * g


def _norm_rows_once(x_ref, g_ref, h_ref):
    @pl.when(pl.program_id(1) == 0)
    def _():
        h_ref[...] = _rmsnorm_rows(x_ref[...], g_ref[...]).astype(BF16)


def _matmul_row_blocks(h_ref, w_ref, epilogue):
    tr = h_ref.shape[0] // MM_ROW_SPLIT
    for r in range(MM_ROW_SPLIT):
        rows = slice(r * tr, (r + 1) * tr)
        epilogue(rows, jnp.dot(h_ref[rows, :], w_ref[...], preferred_element_type=F32))


def _in_proj_kernel(x_ref, g_ref, w_ref, cos_ref, sin_lo_ref, sin_hi_ref, o_ref, h_ref):
    j = pl.program_id(1)
    n_rope_blocks = 2 * DIFF_W // MM_TN
    n_q_blocks = DIFF_W // MM_TN
    _norm_rows_once(x_ref, g_ref, h_ref)

    @pl.when(j < n_rope_blocks)
    def _():
        scale = jnp.where(j < n_q_blocks, LOG2_E * DIFF_QK_DIM ** -0.5, 1.0).astype(F32)

        def rope(rows, acc):
            cos = cos_ref[rows, :] * scale
            sin_lo = sin_lo_ref[rows, :] * scale
            sin_hi = sin_hi_ref[rows, :] * scale
            for c in range(MM_TN // LANES):
                xs = acc[:, c * LANES:(c + 1) * LANES]
                r = (xs * cos
                     + pltpu.roll(xs, LANES - DIFF_QK_DIM // 2, 1) * sin_lo
                     + pltpu.roll(xs, DIFF_QK_DIM // 2, 1) * sin_hi)
                o_ref[rows, c * LANES:(c + 1) * LANES] = r.astype(o_ref.dtype)

        _matmul_row_blocks(h_ref, w_ref, rope)

    @pl.when(j >= n_rope_blocks)
    def _():
        is_na_q = (j >= 3 * DIFF_W // MM_TN) & (j < (3 * DIFF_W + NA_W) // MM_TN)
        scale = jnp.where(is_na_q, LOG2_E * NA_HEAD_DIM ** -0.5, 1.0).astype(F32)

        def scaled(rows, acc):
            o_ref[rows, :] = (acc * scale).astype(o_ref.dtype)

        _matmul_row_blocks(h_ref, w_ref, scaled)


def _mlp_in_kernel(h_ref, w_ref, o_ref, w_bf16_ref):
    @pl.when(pl.program_id(1) == 0)
    def _():
        w_bf16_ref[...] = w_ref[...].astype(BF16)

    def relu2(rows, acc):
        u = jnp.maximum(acc, 0.0)
        o_ref[rows, :] = (u * u).astype(o_ref.dtype)

    _matmul_row_blocks(h_ref, w_bf16_ref, relu2)


def _mlp_in(h, w, layer):
    s, d = h.shape
    f = w.shape[2]
    tm = min(MLP_IN_TM, s)
    return pl.pallas_call(
        _mlp_in_kernel,
        grid=(f // MLP_IN_TN, s // tm),
        in_specs=[pl.BlockSpec((tm, d), lambda j, i: (i, 0)),
                  pl.BlockSpec((None, d, MLP_IN_TN), lambda j, i: (layer, 0, j))],
        out_specs=pl.BlockSpec((tm, MLP_IN_TN), lambda j, i: (i, j)),
        out_shape=jax.ShapeDtypeStruct((s, f), BF16),
        scratch_shapes=[pltpu.VMEM((d, MLP_IN_TN), BF16)],
        compiler_params=_compiler_params(("arbitrary", "arbitrary")),
        name="mlp_in",
    )(h, w)


def _norm_matmul(kernel_fn, x, g, w, layer, tn, extra=(), name=None):
    s, d = x.shape
    n = w.shape[2]
    tm = min(MM_TM, s)
    grid = (s // tm, n // tn)
    extra_specs = [pl.BlockSpec((tm, LANES), lambda i, j: (i, 0)) for _ in extra]
    return pl.pallas_call(
        kernel_fn,
        grid=grid,
        in_specs=[pl.BlockSpec((tm, d), lambda i, j: (i, 0)),
                  pl.BlockSpec((1, d), lambda i, j: (0, 0)),
                  pl.BlockSpec((None, d, tn), lambda i, j: (layer, 0, j))] + extra_specs,
        out_specs=pl.BlockSpec((tm, tn), lambda i, j: (i, j)),
        out_shape=jax.ShapeDtypeStruct((s, n), BF16),
        scratch_shapes=[pltpu.VMEM((tm, d), BF16)],
        compiler_params=_compiler_params(("arbitrary", "arbitrary")),
        name=name,
    )(x, g.reshape(1, d), w, *extra)


def _diff_attn_kernel(lam_ref, q_ref, k_ref, v_ref, lq1_ref, lk1_ref, lq2_ref, lk2_ref,
                      g_ref, o_ref, vt_ref, qpad_ref, s_ref, acc_ref,
                      *, tq, tk, n_chunks, n_blocks):
    for c in range(n_chunks):
        vt_ref[c] = v_ref[c * tk:(c + 1) * tk, :].astype(F32).T.astype(BF16)

    lookahead = ATT_LOOKAHEAD
    lambda_init = lam_ref[0]
    lam = (jnp.exp(jnp.sum(lq1_ref[...] * lk1_ref[...], axis=-1, keepdims=True))
           - jnp.exp(jnp.sum(lq2_ref[...] * lk2_ref[...], axis=-1, keepdims=True))
           + lambda_init)

    def load_queries(i):
        qt = q_ref[pl.ds(pl.multiple_of(i * tq, tq), tq), :].astype(F32).T
        row = lax.broadcasted_iota(jnp.int32, qt.shape, 0)
        qpad_ref[...] = jnp.concatenate([jnp.where(row < DIFF_QK_DIM, qt, 0.0),
                                         jnp.where(row >= DIFF_QK_DIM, qt, 0.0)],
                                        axis=1).astype(BF16)

    def scores(c):
        s = jnp.dot(k_ref[c * tk:(c + 1) * tk, :], qpad_ref[...],
                    preferred_element_type=F32)
        s_ref[c % ATT_SCORE_BUFFERS] = s
        return jnp.max(s, axis=0, keepdims=True)

    def update(c, cmax, m, l):
        m_new = jnp.maximum(m, cmax)
        alpha = jnp.exp2(m - m_new)
        p = jnp.exp2(s_ref[c % ATT_SCORE_BUFFERS] - m_new)
        l = alpha * l + jnp.sum(p, axis=0, keepdims=True)
        pv = jnp.dot(vt_ref[c], p.astype(BF16), preferred_element_type=F32)
        acc_ref[...] = acc_ref[...] * alpha + pv
        return m_new, l

    def start_block(i):
        load_queries(i)
        return tuple(scores(c) for c in range(lookahead))

    def block(i, first_cmax):
        acc_ref[...] = jnp.zeros_like(acc_ref)
        m = jnp.full((1, 2 * tq), MASK_VALUE, F32)
        l = jnp.zeros((1, 2 * tq), F32)
        cmax = dict(enumerate(first_cmax))
        for c in range(n_chunks):
            if c + lookahead < n_chunks:
                cmax[c + lookahead] = scores(c + lookahead)
            m, l = update(c, cmax.pop(c), m, l)
        next_cmax = start_block(jnp.minimum(i + 1, n_blocks - 1))

        o = acc_ref[...] / l
        od = (o[:, :tq] - lam * o[:, tq:]).T
        y = _rmsnorm_rows(od, g_ref[...]) * (1.0 - lambda_init)
        o_ref[pl.ds(pl.multiple_of(i * tq, tq), tq), :] = y.astype(o_ref.dtype)
        return next_cmax

    lax.fori_loop(0, n_blocks, block, start_block(0))


def _diff_attention(proj, lam_init, lq1, lk1, lq2, lk2, subln_g):
    s = proj.shape[0]
    tq = min(ATT_TQ, s)
    tk = min(ATT_TK, s)
    n_chunks = s // tk
    assert n_chunks % ATT_SCORE_BUFFERS == 0, "score slots must line up across query blocks"
    assert ATT_SCORE_BUFFERS >= 2 * ATT_LOOKAHEAD
    kernel_fn = functools.partial(_diff_attn_kernel, tq=tq, tk=tk, n_chunks=n_chunks,
                                  n_blocks=s // tq)
    k_col0 = DIFF_W // LANES
    v_col0 = 2 * DIFF_W // LANES
    vec = lambda n: pl.BlockSpec((1, n), lambda h: (0, 0))
    return pl.pallas_call(
        kernel_fn,
        grid=(N_DIFF_HEADS,),
        in_specs=[pl.BlockSpec(memory_space=pltpu.SMEM),
                  pl.BlockSpec((s, LANES), lambda h: (0, h)),
                  pl.BlockSpec((s, LANES), lambda h: (0, k_col0 + h)),
                  pl.BlockSpec((s, LANES), lambda h: (0, v_col0 + h)),
                  vec(DIFF_QK_DIM), vec(DIFF_QK_DIM), vec(DIFF_QK_DIM), vec(DIFF_QK_DIM),
                  vec(DIFF_V_DIM)],
        out_specs=pl.BlockSpec((s, LANES), lambda h: (0, h)),
        out_shape=jax.ShapeDtypeStruct((s, DIFF_W), BF16),
        scratch_shapes=[pltpu.VMEM((n_chunks, DIFF_V_DIM, tk), BF16),
                        pltpu.VMEM((LANES, 2 * tq), BF16),
                        pltpu.VMEM((ATT_SCORE_BUFFERS, tk, 2 * tq), F32),
                        pltpu.VMEM((DIFF_V_DIM, 2 * tq), F32)],
        compiler_params=_compiler_params(("arbitrary",)),
        name="diff_attention",
    )(lam_init, proj, proj, proj,
      lq1.reshape(1, -1), lk1.reshape(1, -1), lq2.reshape(1, -1), lk2.reshape(1, -1),
      subln_g.reshape(1, -1))


NA_GROUP = 4
NA_SPAN = 3 * NA_GROUP


def _na_span_start(g, n_groups):
    return jnp.clip(g - 1, 0, n_groups - 3)


def _na_attn_kernel(plan_ref, q_ref, k0_ref, k1_ref, k2_ref, v0_ref, v1_ref, v2_ref, bias_ref,
                    o_ref, kwin_ref, vwin_ref, *, n_groups):
    blk = NA_GROUP * GRID_W
    for t, (k_ref, v_ref) in enumerate(((k0_ref, v0_ref), (k1_ref, v1_ref), (k2_ref, v2_ref))):
        kwin_ref[t * blk:(t + 1) * blk, :] = k_ref[...]
        vwin_ref[t * blk:(t + 1) * blk, :] = v_ref[...]

    g = pl.program_id(0)
    variant = jnp.where(g == 0, 0, jnp.where(g == n_groups - 1, 2, 1))
    pairs = NA_SPAN // 2
    lane_half = lax.broadcasted_iota(jnp.int32, (1, 2 * GRID_W), 1) // GRID_W
    starts, row_masks = [], []
    for j in range(NA_GROUP):
        base = (variant * NA_GROUP + j) * 3
        starts.append(plan_ref[base])
        lo, hi = plan_ref[base + 1], plan_ref[base + 2]
        u = [2 * t + lane_half for t in range(pairs)]
        row_masks.append([(ut >= lo) & (ut < hi) for ut in u])

    def bias_rows(h, j):
        tiles = [jnp.where(row_masks[j][t], bias_ref[h, starts[j] + 2 * t], MASK_VALUE)
                 for t in range(pairs)]
        return jnp.concatenate(tiles, axis=1)

    for h in range(N_NA_HEADS):
        cols = slice(h * NA_HEAD_DIM, (h + 1) * NA_HEAD_DIM)
        sc = lax.dot_general(q_ref[:, cols], kwin_ref[:, cols], (((1,), (1,)), ((), ())),
                             preferred_element_type=F32)
        sc = sc + jnp.concatenate([bias_rows(h, j) for j in range(NA_GROUP)], axis=0)
        p = jnp.exp2(sc - jnp.max(sc, axis=-1, keepdims=True))
        l = jnp.sum(p, axis=-1, keepdims=True)
        pv = jnp.dot(p.astype(BF16), vwin_ref[:, cols], preferred_element_type=F32)
        o_ref[:, cols] = pv / l


def _na_window_plan(rows):
    n_groups = rows // NA_GROUP
    kr = NA_WIN_ROWS

    def plan(g):
        start = NA_GROUP * int(np.clip(g - 1, 0, n_groups - 3))
        out = np.full((NA_GROUP, NA_SPAN), -1, np.int64)
        for j in range(NA_GROUP):
            r = NA_GROUP * g + j
            rs = int(np.clip(r - kr // 2, 0, rows - kr))
            for u in range(NA_SPAN):
                if rs <= start + u < rs + kr:
                    out[j, u] = start + u - r + NA_WIN_ROWS - 1
        assert (out >= 0).sum() == NA_GROUP * kr
        return out

    plans = [plan(0), plan(1), plan(n_groups - 1)]
    for g in range(1, n_groups - 1):
        assert (plan(g) == plans[1]).all()
    return plans


def _na_row_windows(rows):
    out = []
    for plan in _na_window_plan(rows):
        for j in range(NA_GROUP):
            valid_u = np.nonzero(plan[j] >= 0)[0]
            assert (np.diff(valid_u) == 1).all() and (np.diff(plan[j, valid_u]) == 1).all()
            out.append((int(plan[j, valid_u[0]] - valid_u[0]), int(valid_u[0]),
                        int(valid_u[-1]) + 1))
    return out


def _na_bias_table(rpb, rows):
    c = np.arange(GRID_W)
    cs = np.clip(c - NA_WIN_COLS // 2, 0, GRID_W - NA_WIN_COLS)
    kc = np.arange(GRID_W)
    col_valid = (kc[None, :] >= cs[:, None]) & (kc[None, :] < cs[:, None] + NA_WIN_COLS)
    pad = GRID_W - NA_WIN_COLS
    rp = jnp.pad(rpb.astype(F32) * LOG2_E, ((0, 0), (0, 0), (pad, pad)))
    by_col = jnp.stack([lax.slice_in_dim(rp, NA_WIN_COLS - 1 - ci + pad,
                                         NA_WIN_COLS - 1 - ci + pad + GRID_W, axis=2)
                        for ci in range(GRID_W)], axis=2)
    by_col = jnp.where(col_valid[None, None], by_col, MASK_VALUE)
    windows = _na_row_windows(rows)
    front = max(0, -min(w[0] for w in windows))
    back = max(0, max(w[0] for w in windows) + NA_SPAN - by_col.shape[1])
    padded = jnp.pad(by_col, ((0, 0), (front, back + 1), (0, 0), (0, 0)))
    table = jnp.concatenate([padded[:, :-1], padded[:, 1:]], axis=-1)
    plan = np.array([[w[0] + front, w[1], w[2]] for w in windows], np.int32).reshape(-1)
    return table, jnp.asarray(plan)


def _na_attention(proj, bias, plan, layer):
    s = proj.shape[0]
    rows = s // GRID_W
    assert rows % NA_GROUP == 0 and rows >= NA_SPAN and NA_SPAN >= NA_GROUP + NA_WIN_ROWS - 1
    assert NA_SPAN % 2 == 0
    n_groups = rows // NA_GROUP
    blk = NA_GROUP * GRID_W
    q_col = 3 * DIFF_W // NA_W
    k_col, v_col = q_col + 1, q_col + 2
    bias = bias.reshape((-1, N_NA_HEADS) + bias.shape[1:])

    def kv_spec(t, col):
        return pl.BlockSpec((blk, NA_W),
                            lambda g, plan_ref: (_na_span_start(g, n_groups) + t, col))

    grid_spec = pltpu.PrefetchScalarGridSpec(
        num_scalar_prefetch=1,
        grid=(n_groups,),
        in_specs=[pl.BlockSpec((blk, NA_W), lambda g, plan_ref: (g, q_col))]
                 + [kv_spec(t, k_col) for t in range(3)]
                 + [kv_spec(t, v_col) for t in range(3)]
                 + [pl.BlockSpec((None,) + bias.shape[1:],
                                 lambda g, plan_ref: (layer, 0, 0, 0, 0))],
        out_specs=pl.BlockSpec((blk, NA_W), lambda g, plan_ref: (g, 0)),
        scratch_shapes=[pltpu.VMEM((NA_SPAN * GRID_W, NA_W), BF16),
                        pltpu.VMEM((NA_SPAN * GRID_W, NA_W), BF16)])
    return pl.pallas_call(
        functools.partial(_na_attn_kernel, n_groups=n_groups),
        grid_spec=grid_spec,
        out_shape=jax.ShapeDtypeStruct((s, NA_W), F32),
        compiler_params=_compiler_params(("arbitrary",)),
        name="na_attention",
    )(plan, *([proj] * 7), bias)


def _out_proj_kernel(a_ref, b_ref, na_g_ref, w_ref, x_ref, mlp_g_ref, o_ref, h_ref):
    ka = a_ref.shape[1]
    tr = a_ref.shape[0] // MM_ROW_SPLIT
    for r in range(MM_ROW_SPLIT):
        rows = slice(r * tr, (r + 1) * tr)
        b = _rmsnorm_rows(b_ref[rows, :], na_g_ref[...]).astype(BF16)
        acc = jnp.dot(a_ref[rows, :], w_ref[:ka, :], preferred_element_type=F32)
        acc += jnp.dot(b, w_ref[ka:, :], preferred_element_type=F32)
        x_new = x_ref[rows, :] + acc
        o_ref[rows, :] = x_new
        h_ref[rows, :] = _rmsnorm_rows(x_new, mlp_g_ref[...]).astype(h_ref.dtype)


def _out_proj(o_diff, o_na, na_g, w, layer, x, mlp_g):
    s, d = x.shape
    tm = min(OUT_PROJ_TM, s)
    ka, kb = o_diff.shape[1], o_na.shape[1]
    row_block = lambda width: pl.BlockSpec((tm, width), lambda i: (i, 0))
    vec = lambda width: pl.BlockSpec((1, width), lambda i: (0, 0))
    return pl.pallas_call(
        _out_proj_kernel,
        grid=(s // tm,),
        in_specs=[row_block(ka), row_block(kb), vec(kb),
                  pl.BlockSpec((None, ka + kb, d), lambda i: (layer, 0, 0)),
                  row_block(d), vec(d)],
        out_specs=(row_block(d), row_block(d)),
        out_shape=(jax.ShapeDtypeStruct((s, d), F32), jax.ShapeDtypeStruct((s, d), BF16)),
        compiler_params=_compiler_params(("arbitrary",)),
        name="out_proj",
    )(o_diff, o_na, na_g.reshape(1, kb), w, x, mlp_g.reshape(1, d))


def _mlp_out_kernel(u_ref, w_ref, x_ref, *rest, final_norm):
    o_ref = rest[-1]
    k = pl.program_id(1)

    @pl.when(k == 0)
    def _():
        o_ref[...] = x_ref[...]

    o_ref[...] += jnp.dot(u_ref[...], w_ref[...], preferred_element_type=F32)

    if final_norm:
        @pl.when(k == pl.num_programs(1) - 1)
        def _():
            o_ref[...] = _rmsnorm_rows(o_ref[...], rest[0][...])


def _mlp_out(u, w, layer, x, final_g=None):
    s, d = x.shape
    f = u.shape[1]
    tm = min(MM_TM, s)
    extra, extra_specs = (), []
    if final_g is not None:
        extra = (final_g.reshape(1, d),)
        extra_specs = [pl.BlockSpec((1, d), lambda i, k: (0, 0))]
    return pl.pallas_call(
        functools.partial(_mlp_out_kernel, final_norm=final_g is not None),
        grid=(s // tm, f // MLP_OUT_TK),
        in_specs=[pl.BlockSpec((tm, MLP_OUT_TK), lambda i, k: (i, k)),
                  pl.BlockSpec((None, MLP_OUT_TK, d), lambda i, k: (layer, k, 0)),
                  pl.BlockSpec((tm, d), lambda i, k: (i, 0))] + extra_specs,
        out_specs=pl.BlockSpec((tm, d), lambda i, k: (i, 0)),
        out_shape=jax.ShapeDtypeStruct((s, d), F32),
        compiler_params=_compiler_params(("arbitrary", "arbitrary")),
        name="mlp_out_final" if final_g is not None else "mlp_out",
    )(u, w, x, *extra)


def _rope_tables(seq):
    inv_freq = 1.0 / (ROPE_THETA ** (jnp.arange(0, DIFF_QK_DIM, 2, dtype=F32) / DIFF_QK_DIM))
    ang = jnp.arange(seq, dtype=F32)[:, None] * inv_freq[None, :]
    cos, sin = jnp.cos(ang), jnp.sin(ang)
    zero = jnp.zeros_like(sin)
    reps = LANES // DIFF_QK_DIM
    cos_t = jnp.tile(cos, (1, 2 * reps))
    sin_lo = jnp.tile(jnp.concatenate([-sin, zero], axis=1), (1, reps))
    sin_hi = jnp.tile(jnp.concatenate([zero, sin], axis=1), (1, reps))
    return cos_t, sin_lo, sin_hi


def kernel(x, attn_norm, w_in, lambda_q1, lambda_k1, lambda_q2, lambda_k2, diff_subln, na_norm,
           na_rpb, w_out, mlp_norm, w_mlp_in, w_mlp_out, final_norm):
    b, s, d = x.shape
    depth = w_in.shape[0]
    rope = _rope_tables(s)
    na_bias, na_plan = _na_bias_table(
        na_rpb.reshape((depth * N_NA_HEADS,) + na_rpb.shape[2:]), s // GRID_W)
    w_in, w_out, w_mlp_out = (w.astype(BF16) for w in (w_in, w_out, w_mlp_out))
    w_mlp_in = w_mlp_in.astype(F32)
    outs = []
    for bi in range(b):
        xb = x[bi].astype(F32)
        for l in range(depth):
            lambda_init = 0.8 - 0.6 * math.exp(-0.3 * l)
            proj = _norm_matmul(_in_proj_kernel, xb, attn_norm[l].astype(F32), w_in, l, MM_TN,
                                rope, name="in_proj")
            o_diff = _diff_attention(proj, jnp.full((1,), lambda_init, F32),
                                     lambda_q1[l].astype(F32), lambda_k1[l].astype(F32),
                                     lambda_q2[l].astype(F32), lambda_k2[l].astype(F32),
                                     diff_subln[l].astype(F32))
            o_na = _na_attention(proj, na_bias, na_plan, l)
            xb, h_mlp = _out_proj(o_diff, o_na, na_norm[l].astype(F32), w_out, l, xb,
                                  mlp_norm[l].astype(F32))
            u = _mlp_in(h_mlp, w_mlp_in, l)
            xb = _mlp_out(u, w_mlp_out, l, xb,
                          final_norm.astype(F32) if l == depth - 1 else None)
        outs.append(xb)
    out = outs[0][None] if b == 1 else jnp.stack(outs)
    return out.astype(x.dtype)
```

```python
import functools
import math

import jax
import jax.numpy as jnp
import numpy as np
from jax import lax
from jax.experimental import pallas as pl
from jax.experimental.pallas import tpu as pltpu

F32 = jnp.float32
BF16 = jnp.bfloat16

GRID_W = 64
N_DIFF_HEADS = 8
DIFF_QK_DIM = 64
DIFF_V_DIM = 128
DIFF_W = N_DIFF_HEADS * DIFF_V_DIM
N_NA_HEADS = 8
NA_HEAD_DIM = 128
NA_W = N_NA_HEADS * NA_HEAD_DIM
NA_WIN_ROWS = 8
NA_WIN_COLS = 16
ROPE_THETA = 10000.0
EPS = 1e-6
MASK_VALUE = -1e30
LOG2_E = math.log2(math.e)

LANES = 128
VMEM_LIMIT_BYTES = 56 * 1024 * 1024

MM_TM = 1024
MM_TN = 1024
MLP_IN_TM = 2048
MLP_IN_TN = 1024
OUT_PROJ_TM = 512
OUT_PROJ_ROW_SPLIT = 2
MM_ROW_SPLIT = 4
MLP_OUT_TK = 1024
ATT_TQ = 256
ATT_TK = 512
ATT_LOOKAHEAD = 2
ATT_SCORE_BUFFERS = 4


def _compiler_params(semantics, flags=None):
    return pltpu.CompilerParams(dimension_semantics=semantics,
                                vmem_limit_bytes=VMEM_LIMIT_BYTES, flags=flags)


def _rmsnorm_rows(x, g):
    return x * lax.rsqrt(jnp.mean(x * x, axis=-1, keepdims=True) + EPS) * g


def _norm_rows_once(x_ref, g_ref, h_ref):
    @pl.when(pl.program_id(1) == 0)
    def _():
        h_ref[...] = _rmsnorm_rows(x_ref[...], g_ref[...]).astype(BF16)


def _matmul_row_blocks(h_ref, w_ref, epilogue):
    tr = h_ref.shape[0] // MM_ROW_SPLIT
    for r in range(MM_ROW_SPLIT):
        rows = slice(r * tr, (r + 1) * tr)
        epilogue(rows, jnp.dot(h_ref[rows, :], w_ref[...], preferred_element_type=F32))


def _in_proj_kernel(x_ref, g_ref, w_ref, cos_ref, sin_lo_ref, sin_hi_ref, o_ref, h_ref):
    j = pl.program_id(1)
    n_rope_blocks = 2 * DIFF_W // MM_TN
    n_q_blocks = DIFF_W // MM_TN
    _norm_rows_once(x_ref, g_ref, h_ref)

    @pl.when(j < n_rope_blocks)
    def _():
        scale = jnp.where(j < n_q_blocks, LOG2_E * DIFF_QK_DIM ** -0.5, 1.0).astype(F32)

        def rope(rows, acc):
            cos = cos_ref[rows, :] * scale
            sin_lo = sin_lo_ref[rows, :] * scale
            sin_hi = sin_hi_ref[rows, :] * scale
            for c in range(MM_TN // LANES):
                xs = acc[:, c * LANES:(c + 1) * LANES]
                r = (xs * cos
                     + pltpu.roll(xs, LANES - DIFF_QK_DIM // 2, 1) * sin_lo
                     + pltpu.roll(xs, DIFF_QK_DIM // 2, 1) * sin_hi)
                o_ref[rows, c * LANES:(c + 1) * LANES] = r.astype(o_ref.dtype)

        _matmul_row_blocks(h_ref, w_ref, rope)

    @pl.when(j >= n_rope_blocks)
    def _():
        is_na_q = (j >= 3 * DIFF_W // MM_TN) & (j < (3 * DIFF_W + NA_W) // MM_TN)
        scale = jnp.where(is_na_q, LOG2_E * NA_HEAD_DIM ** -0.5, 1.0).astype(F32)

        def scaled(rows, acc):
            o_ref[rows, :] = (acc * scale).astype(o_ref.dtype)

        _matmul_row_blocks(h_ref, w_ref, scaled)


def _mlp_in_kernel(h_ref, w_ref, o_ref, w_bf16_ref):
    @pl.when(pl.program_id(1) == 0)
    def _():
        w_bf16_ref[...] = w_ref[...].astype(BF16)

    def relu2(rows, acc):
        u = jnp.maximum(acc, 0.0)
        o_ref[rows, :] = (u * u).astype(o_ref.dtype)

    _matmul_row_blocks(h_ref, w_bf16_ref, relu2)


def _mlp_in(h, w, layer):
    s, d = h.shape
    f = w.shape[2]
    tm = min(MLP_IN_TM, s)
    return pl.pallas_call(
        _mlp_in_kernel,
        grid=(f // MLP_IN_TN, s // tm),
        in_specs=[pl.BlockSpec((tm, d), lambda j, i: (i, 0)),
                  pl.BlockSpec((None, d, MLP_IN_TN), lambda j, i: (layer, 0, j))],
        out_specs=pl.BlockSpec((tm, MLP_IN_TN), lambda j, i: (i, j)),
        out_shape=jax.ShapeDtypeStruct((s, f), BF16),
        scratch_shapes=[pltpu.VMEM((d, MLP_IN_TN), BF16)],
        compiler_params=_compiler_params(("arbitrary", "arbitrary")),
        name="mlp_in",
    )(h, w)


def _norm_matmul(kernel_fn, x, g, w, layer, tn, extra=(), name=None):
    s, d = x.shape
    n = w.shape[2]
    tm = min(MM_TM, s)
    grid = (s // tm, n // tn)
    extra_specs = [pl.BlockSpec((tm, LANES), lambda i, j: (i, 0)) for _ in extra]
    return pl.pallas_call(
        kernel_fn,
        grid=grid,
        in_specs=[pl.BlockSpec((tm, d), lambda i, j: (i, 0)),
                  pl.BlockSpec((1, d), lambda i, j: (0, 0)),
                  pl.BlockSpec((None, d, tn), lambda i, j: (layer, 0, j))] + extra_specs,
        out_specs=pl.BlockSpec((tm, tn), lambda i, j: (i, j)),
        out_shape=jax.ShapeDtypeStruct((s, n), BF16),
        scratch_shapes=[pltpu.VMEM((tm, d), BF16)],
        compiler_params=_compiler_params(("arbitrary", "arbitrary")),
        name=name,
    )(x, g.reshape(1, d), w, *extra)


def _diff_attn_kernel(lam_ref, q_ref, k_ref, v_ref, lq1_ref, lk1_ref, lq2_ref, lk2_ref,
                      g_ref, o_ref, vt_ref, qpad_ref, s_ref, acc_ref,
                      *, tq, tk, n_chunks, n_blocks):
    for c in range(n_chunks):
        vt_ref[c] = v_ref[c * tk:(c + 1) * tk, :].astype(F32).T.astype(BF16)

    lookahead = ATT_LOOKAHEAD
    lambda_init = lam_ref[0]
    lam = (jnp.exp(jnp.sum(lq1_ref[...] * lk1_ref[...], axis=-1, keepdims=True))
           - jnp.exp(jnp.sum(lq2_ref[...] * lk2_ref[...], axis=-1, keepdims=True))
           + lambda_init)

    def load_queries(i):
        qt = q_ref[pl.ds(pl.multiple_of(i * tq, tq), tq), :].astype(F32).T
        row = lax.broadcasted_iota(jnp.int32, qt.shape, 0)
        qpad_ref[...] = jnp.concatenate([jnp.where(row < DIFF_QK_DIM, qt, 0.0),
                                         jnp.where(row >= DIFF_QK_DIM, qt, 0.0)],
                                        axis=1).astype(BF16)

    def scores(c):
        s = jnp.dot(k_ref[c * tk:(c + 1) * tk, :], qpad_ref[...],
                    preferred_element_type=F32)
        s_ref[c % ATT_SCORE_BUFFERS] = s
        return jnp.max(s, axis=0, keepdims=True)

    def softmax_chunk(c, cmax, m, l):
        m_new = jnp.maximum(m, cmax)
        alpha = jnp.exp2(m - m_new)
        p = jnp.exp2(s_ref[c % ATT_SCORE_BUFFERS] - m_new)
        l = alpha * l + jnp.sum(p, axis=0, keepdims=True)
        return m_new, l, alpha, p.astype(BF16)

    def accumulate(c, alpha, p):
        pv = jnp.dot(vt_ref[c], p, preferred_element_type=F32)
        acc_ref[...] = acc_ref[...] * alpha + pv

    def start_block(i):
        load_queries(i)
        return tuple(scores(c) for c in range(lookahead))

    def block(i, first_cmax):
        acc_ref[...] = jnp.zeros_like(acc_ref)
        m = jnp.full((1, 2 * tq), MASK_VALUE, F32)
        l = jnp.zeros((1, 2 * tq), F32)
        cmax = dict(enumerate(first_cmax))
        for c in range(n_chunks):
            if c + lookahead < n_chunks:
                cmax[c + lookahead] = scores(c + lookahead)
            m, l, alpha, p = softmax_chunk(c, cmax.pop(c), m, l)
            accumulate(c, alpha, p)
        next_cmax = start_block(jnp.minimum(i + 1, n_blocks - 1))

        o = acc_ref[...] / l
        od = (o[:, :tq] - lam * o[:, tq:]).T
        y = _rmsnorm_rows(od, g_ref[...]) * (1.0 - lambda_init)
        o_ref[pl.ds(pl.multiple_of(i * tq, tq), tq), :] = y.astype(o_ref.dtype)
        return next_cmax

    lax.fori_loop(0, n_blocks, block, start_block(0))


def _diff_attention(proj, lam_init, lq1, lk1, lq2, lk2, subln_g):
    s = proj.shape[0]
    tq = min(ATT_TQ, s)
    tk = min(ATT_TK, s)
    n_chunks = s // tk
    assert n_chunks % ATT_SCORE_BUFFERS == 0, "score slots must line up across query blocks"
    assert ATT_SCORE_BUFFERS >= 2 * ATT_LOOKAHEAD
    kernel_fn = functools.partial(_diff_attn_kernel, tq=tq, tk=tk, n_chunks=n_chunks,
                                  n_blocks=s // tq)
    k_col0 = DIFF_W // LANES
    v_col0 = 2 * DIFF_W // LANES
    vec = lambda n: pl.BlockSpec((1, n), lambda h: (0, 0))
    return pl.pallas_call(
        kernel_fn,
        grid=(N_DIFF_HEADS,),
        in_specs=[pl.BlockSpec(memory_space=pltpu.SMEM),
                  pl.BlockSpec((s, LANES), lambda h: (0, h)),
                  pl.BlockSpec((s, LANES), lambda h: (0, k_col0 + h)),
                  pl.BlockSpec((s, LANES), lambda h: (0, v_col0 + h)),
                  vec(DIFF_QK_DIM), vec(DIFF_QK_DIM), vec(DIFF_QK_DIM), vec(DIFF_QK_DIM),
                  vec(DIFF_V_DIM)],
        out_specs=pl.BlockSpec((s, LANES), lambda h: (0, h)),
        out_shape=jax.ShapeDtypeStruct((s, DIFF_W), BF16),
        scratch_shapes=[pltpu.VMEM((n_chunks, DIFF_V_DIM, tk), BF16),
                        pltpu.VMEM((LANES, 2 * tq), BF16),
                        pltpu.VMEM((ATT_SCORE_BUFFERS, tk, 2 * tq), F32),
                        pltpu.VMEM((DIFF_V_DIM, 2 * tq), F32)],
        compiler_params=_compiler_params(("arbitrary",)),
        name="diff_attention",
    )(lam_init, proj, proj, proj,
      lq1.reshape(1, -1), lk1.reshape(1, -1), lq2.reshape(1, -1), lk2.reshape(1, -1),
      subln_g.reshape(1, -1))


NA_GROUP = 4
NA_SPAN = 3 * NA_GROUP
NA_LOOKAHEAD = 2


def _na_span_start(g, n_groups):
    return jnp.clip(g - 1, 0, n_groups - 3)


def _na_attn_kernel(plan_ref, q_ref, k0_ref, k1_ref, k2_ref, v0_ref, v1_ref, v2_ref, bias_ref,
                    o_ref, kwin_ref, vwin_ref, *, n_groups):
    blk = NA_GROUP * GRID_W
    for t, (k_ref, v_ref) in enumerate(((k0_ref, v0_ref), (k1_ref, v1_ref), (k2_ref, v2_ref))):
        kwin_ref[t * blk:(t + 1) * blk, :] = k_ref[...]
        vwin_ref[t * blk:(t + 1) * blk, :] = v_ref[...]

    g = pl.program_id(0)
    variant = jnp.where(g == 0, 0, jnp.where(g == n_groups - 1, 2, 1))
    pairs = NA_SPAN // 2
    lane_half = lax.broadcasted_iota(jnp.int32, (1, 2 * GRID_W), 1) // GRID_W
    starts, row_masks = [], []
    for j in range(NA_GROUP):
        base = (variant * NA_GROUP + j) * 3
        starts.append(plan_ref[base])
        lo, hi = plan_ref[base + 1], plan_ref[base + 2]
        u = [2 * t + lane_half for t in range(pairs)]
        row_masks.append([(ut >= lo) & (ut < hi) for ut in u])

    def bias_rows(h, j):
        tiles = [jnp.where(row_masks[j][t], bias_ref[h, starts[j] + 2 * t], MASK_VALUE)
                 for t in range(pairs)]
        return jnp.concatenate(tiles, axis=1)

    def head_cols(h):
        return slice(h * NA_HEAD_DIM, (h + 1) * NA_HEAD_DIM)

    def scores(h):
        sc = lax.dot_general(q_ref[:, head_cols(h)], kwin_ref[:, head_cols(h)],
                             (((1,), (1,)), ((), ())),
                             preferred_element_type=F32)
        return sc + jnp.concatenate([bias_rows(h, j) for j in range(NA_GROUP)], axis=0)

    pending = {h: scores(h) for h in range(NA_LOOKAHEAD)}
    for h in range(N_NA_HEADS):
        if h + NA_LOOKAHEAD < N_NA_HEADS:
            pending[h + NA_LOOKAHEAD] = scores(h + NA_LOOKAHEAD)
        sc = pending.pop(h)
        p = jnp.exp2(sc - jnp.max(sc, axis=-1, keepdims=True))
        l = jnp.sum(p, axis=-1, keepdims=True)
        pv = jnp.dot(p.astype(BF16), vwin_ref[:, head_cols(h)], preferred_element_type=F32)
        o_ref[:, head_cols(h)] = pv / l


def _na_window_plan(rows):
    n_groups = rows // NA_GROUP
    kr = NA_WIN_ROWS

    def plan(g):
        start = NA_GROUP * int(np.clip(g - 1, 0, n_groups - 3))
        out = np.full((NA_GROUP, NA_SPAN), -1, np.int64)
        for j in range(NA_GROUP):
            r = NA_GROUP * g + j
            rs = int(np.clip(r - kr // 2, 0, rows - kr))
            for u in range(NA_SPAN):
                if rs <= start + u < rs + kr:
                    out[j, u] = start + u - r + NA_WIN_ROWS - 1
        assert (out >= 0).sum() == NA_GROUP * kr
        return out

    plans = [plan(0), plan(1), plan(n_groups - 1)]
    for g in range(1, n_groups - 1):
        assert (plan(g) == plans[1]).all()
    return plans


def _na_row_windows(rows):
    out = []
    for plan in _na_window_plan(rows):
        for j in range(NA_GROUP):
            valid_u = np.nonzero(plan[j] >= 0)[0]
            assert (np.diff(valid_u) == 1).all() and (np.diff(plan[j, valid_u]) == 1).all()
            out.append((int(plan[j, valid_u[0]] - valid_u[0]), int(valid_u[0]),
                        int(valid_u[-1]) + 1))
    return out


def _na_bias_table(rpb, rows):
    c = np.arange(GRID_W)
    cs = np.clip(c - NA_WIN_COLS // 2, 0, GRID_W - NA_WIN_COLS)
    kc = np.arange(GRID_W)
    col_valid = (kc[None, :] >= cs[:, None]) & (kc[None, :] < cs[:, None] + NA_WIN_COLS)
    pad = GRID_W - NA_WIN_COLS
    rp = jnp.pad(rpb.astype(F32) * LOG2_E, ((0, 0), (0, 0), (pad, pad)))
    by_col = jnp.stack([lax.slice_in_dim(rp, NA_WIN_COLS - 1 - ci + pad,
                                         NA_WIN_COLS - 1 - ci + pad + GRID_W, axis=2)
                        for ci in range(GRID_W)], axis=2)
    by_col = jnp.where(col_valid[None, None], by_col, MASK_VALUE)
    windows = _na_row_windows(rows)
    front = max(0, -min(w[0] for w in windows))
    back = max(0, max(w[0] for w in windows) + NA_SPAN - by_col.shape[1])
    padded = jnp.pad(by_col, ((0, 0), (front, back + 1), (0, 0), (0, 0)))
    table = jnp.concatenate([padded[:, :-1], padded[:, 1:]], axis=-1)
    plan = np.array([[w[0] + front, w[1], w[2]] for w in windows], np.int32).reshape(-1)
    return table, jnp.asarray(plan)


def _na_attention(proj, bias, plan, layer):
    s = proj.shape[0]
    rows = s // GRID_W
    assert rows % NA_GROUP == 0 and rows >= NA_SPAN and NA_SPAN >= NA_GROUP + NA_WIN_ROWS - 1
    assert NA_SPAN % 2 == 0
    n_groups = rows // NA_GROUP
    blk = NA_GROUP * GRID_W
    q_col = 3 * DIFF_W // NA_W
    k_col, v_col = q_col + 1, q_col + 2
    bias = bias.reshape((-1, N_NA_HEADS) + bias.shape[1:])

    def kv_spec(t, col):
        return pl.BlockSpec((blk, NA_W),
                            lambda g, plan_ref: (_na_span_start(g, n_groups) + t, col))

    grid_spec = pltpu.PrefetchScalarGridSpec(
        num_scalar_prefetch=1,
        grid=(n_groups,),
        in_specs=[pl.BlockSpec((blk, NA_W), lambda g, plan_ref: (g, q_col))]
                 + [kv_spec(t, k_col) for t in range(3)]
                 + [kv_spec(t, v_col) for t in range(3)]
                 + [pl.BlockSpec((None,) + bias.shape[1:],
                                 lambda g, plan_ref: (layer, 0, 0, 0, 0))],
        out_specs=pl.BlockSpec((blk, NA_W), lambda g, plan_ref: (g, 0)),
        scratch_shapes=[pltpu.VMEM((NA_SPAN * GRID_W, NA_W), BF16),
                        pltpu.VMEM((NA_SPAN * GRID_W, NA_W), BF16)])
    return pl.pallas_call(
        functools.partial(_na_attn_kernel, n_groups=n_groups),
        grid_spec=grid_spec,
        out_shape=jax.ShapeDtypeStruct((s, NA_W), F32),
        compiler_params=_compiler_params(("arbitrary",)),
        name="na_attention",
    )(plan, *([proj] * 7), bias)


def _out_proj_kernel(a_ref, b_ref, na_g_ref, w_ref, x_ref, mlp_g_ref, o_ref, h_ref):
    ka = a_ref.shape[1]
    tr = a_ref.shape[0] // OUT_PROJ_ROW_SPLIT
    row_slices = [slice(r * tr, (r + 1) * tr) for r in range(OUT_PROJ_ROW_SPLIT)]

    def normed_b(r):
        return _rmsnorm_rows(b_ref[row_slices[r], :], na_g_ref[...]).astype(BF16)

    b = normed_b(0)
    for r, rows in enumerate(row_slices):
        b_next = normed_b(r + 1) if r + 1 < OUT_PROJ_ROW_SPLIT else None
        acc = jnp.dot(a_ref[rows, :], w_ref[:ka, :], preferred_element_type=F32)
        acc += jnp.dot(b, w_ref[ka:, :], preferred_element_type=F32)
        x_new = x_ref[rows, :] + acc
        o_ref[rows, :] = x_new
        h_ref[rows, :] = _rmsnorm_rows(x_new, mlp_g_ref[...]).astype(h_ref.dtype)
        b = b_next


def _out_proj(o_diff, o_na, na_g, w, layer, x, mlp_g):
    s, d = x.shape
    tm = min(OUT_PROJ_TM, s)
    ka, kb = o_diff.shape[1], o_na.shape[1]
    row_block = lambda width: pl.BlockSpec((tm, width), lambda i: (i, 0))
    vec = lambda width: pl.BlockSpec((1, width), lambda i: (0, 0))
    return pl.pallas_call(
        _out_proj_kernel,
        grid=(s // tm,),
        in_specs=[row_block(ka), row_block(kb), vec(kb),
                  pl.BlockSpec((None, ka + kb, d), lambda i: (layer, 0, 0)),
                  row_block(d), vec(d)],
        out_specs=(row_block(d), row_block(d)),
        out_shape=(jax.ShapeDtypeStruct((s, d), F32), jax.ShapeDtypeStruct((s, d), BF16)),
        compiler_params=_compiler_params(("arbitrary",)),
        name="out_proj",
    )(o_diff, o_na, na_g.reshape(1, kb), w, x, mlp_g.reshape(1, d))


def _mlp_out_kernel(u_ref, w_ref, x_ref, *rest, final_norm):
    o_ref = rest[-1]
    k = pl.program_id(1)

    @pl.when(k == 0)
    def _():
        o_ref[...] = x_ref[...]

    o_ref[...] += jnp.dot(u_ref[...], w_ref[...], preferred_element_type=F32)

    if final_norm:
        @pl.when(k == pl.num_programs(1) - 1)
        def _():
            o_ref[...] = _rmsnorm_rows(o_ref[...], rest[0][...])


def _mlp_out(u, w, layer, x, final_g=None):
    s, d = x.shape
    f = u.shape[1]
    tm = min(MM_TM, s)
    extra, extra_specs = (), []
    if final_g is not None:
        extra = (final_g.reshape(1, d),)
        extra_specs = [pl.BlockSpec((1, d), lambda i, k: (0, 0))]
    return pl.pallas_call(
        functools.partial(_mlp_out_kernel, final_norm=final_g is not None),
        grid=(s // tm, f // MLP_OUT_TK),
        in_specs=[pl.BlockSpec((tm, MLP_OUT_TK), lambda i, k: (i, k)),
                  pl.BlockSpec((None, MLP_OUT_TK, d), lambda i, k: (layer, k, 0)),
                  pl.BlockSpec((tm, d), lambda i, k: (i, 0))] + extra_specs,
        out_specs=pl.BlockSpec((tm, d), lambda i, k: (i, 0)),
        out_shape=jax.ShapeDtypeStruct((s, d), F32),
        compiler_params=_compiler_params(("arbitrary", "arbitrary")),
        name="mlp_out_final" if final_g is not None else "mlp_out",
    )(u, w, x, *extra)


def _rope_tables(seq):
    inv_freq = 1.0 / (ROPE_THETA ** (jnp.arange(0, DIFF_QK_DIM, 2, dtype=F32) / DIFF_QK_DIM))
    ang = jnp.arange(seq, dtype=F32)[:, None] * inv_freq[None, :]
    cos, sin = jnp.cos(ang), jnp.sin(ang)
    zero = jnp.zeros_like(sin)
    reps = LANES // DIFF_QK_DIM
    cos_t = jnp.tile(cos, (1, 2 * reps))
    sin_lo = jnp.tile(jnp.concatenate([-sin, zero], axis=1), (1, reps))
    sin_hi = jnp.tile(jnp.concatenate([zero, sin], axis=1), (1, reps))
    return cos_t, sin_lo, sin_hi


def kernel(x, attn_norm, w_in, lambda_q1, lambda_k1, lambda_q2, lambda_k2, diff_subln, na_norm,
           na_rpb, w_out, mlp_norm, w_mlp_in, w_mlp_out, final_norm):
    b, s, d = x.shape
    depth = w_in.shape[0]
    rope = _rope_tables(s)
    na_bias, na_plan = _na_bias_table(
        na_rpb.reshape((depth * N_NA_HEADS,) + na_rpb.shape[2:]), s // GRID_W)
    w_in, w_out, w_mlp_out = (w.astype(BF16) for w in (w_in, w_out, w_mlp_out))
    w_mlp_in = w_mlp_in.astype(F32)
    outs = []
    for bi in range(b):
        xb = x[bi].astype(F32)
        for l in range(depth):
            lambda_init = 0.8 - 0.6 * math.exp(-0.3 * l)
            proj = _norm_matmul(_in_proj_kernel, xb, attn_norm[l].astype(F32), w_in, l, MM_TN,
                                rope, name="in_proj")
            o_diff = _diff_attention(proj, jnp.full((1,), lambda_init, F32),
                                     lambda_q1[l].astype(F32), lambda_k1[l].astype(F32),
                                     lambda_q2[l].astype(F32), lambda_k2[l].astype(F32),
                                     diff_subln[l].astype(F32))
            o_na = _na_attention(proj, na_bias, na_plan, l)
            xb, h_mlp = _out_proj(o_diff, o_na, na_norm[l].astype(F32), w_out, l, xb,
                                  mlp_norm[l].astype(F32))
            u = _mlp_in(h_mlp, w_mlp_in, l)
            xb = _mlp_out(u, w_mlp_out, l, xb,
                          final_norm.astype(F32) if l == depth - 1 else None)
        outs.append(xb)
    out = outs[0][None] if b == 1 else jnp.stack(outs)
    return out.astype(x.dtype)
```

```python
import functools
import math

import jax
import jax.numpy as jnp
import numpy as np
from jax import lax
from jax.experimental import pallas as pl
from jax.experimental.pallas import tpu as pltpu

F32 = jnp.float32
BF16 = jnp.bfloat16

GRID_W = 64
N_DIFF_HEADS = 8
DIFF_QK_DIM = 64
DIFF_V_DIM = 128
DIFF_W = N_DIFF_HEADS * DIFF_V_DIM
N_NA_HEADS = 8
NA_HEAD_DIM = 128
NA_W = N_NA_HEADS * NA_HEAD_DIM
NA_WIN_ROWS = 8
NA_WIN_COLS = 16
ROPE_THETA = 10000.0
EPS = 1e-6
MASK_VALUE = -1e30
LOG2_E = math.log2(math.e)

LANES = 128
VMEM_LIMIT_BYTES = 56 * 1024 * 1024

MM_TM = 1024
MM_TN = 1024
MLP_IN_TM = 2048
MLP_IN_TN = 1024
OUT_PROJ_TM = 512
OUT_PROJ_ROW_SPLIT = 2
MM_ROW_SPLIT = 4
MLP_OUT_TK = 1024
ATT_TQ = 256
ATT_TK = 512
ATT_LOOKAHEAD = 2
ATT_SCORE_BUFFERS = 4


def _compiler_params(semantics, flags=None):
    return pltpu.CompilerParams(dimension_semantics=semantics,
                                vmem_limit_bytes=VMEM_LIMIT_BYTES, flags=flags)


def _rmsnorm_rows(x, g):
    return x * lax.rsqrt(jnp.mean(x * x, axis=-1, keepdims=True) + EPS) * g


def _rmsnorm_cast_kernel(x_ref, g_ref, o_ref):
    o_ref[...] = _rmsnorm_rows(x_ref[...], g_ref[...]).astype(o_ref.dtype)


def _rmsnorm_cast(x, g):
    s, d = x.shape
    tm = min(MM_TM, s)
    return pl.pallas_call(
        _rmsnorm_cast_kernel,
        grid=(s // tm,),
        in_specs=[pl.BlockSpec((tm, d), lambda i: (i, 0)), pl.BlockSpec((1, d), lambda i: (0, 0))],
        out_specs=pl.BlockSpec((tm, d), lambda i: (i, 0)),
        out_shape=jax.ShapeDtypeStruct((s, d), BF16),
        compiler_params=_compiler_params(("arbitrary",)),
        name="first_norm",
    )(x, g.reshape(1, d))


def _cast_weights_once(w_ref, w_bf16_ref):
    @pl.when(pl.program_id(1) == 0)
    def _():
        w_bf16_ref[...] = w_ref[...].astype(BF16)


def _matmul_row_blocks(h_ref, w_ref, epilogue):
    tr = h_ref.shape[0] // MM_ROW_SPLIT
    for r in range(MM_ROW_SPLIT):
        rows = slice(r * tr, (r + 1) * tr)
        epilogue(rows, jnp.dot(h_ref[rows, :], w_ref[...], preferred_element_type=F32))


def _in_proj_kernel(h_ref, w_f32_ref, cos_ref, sin_lo_ref, sin_hi_ref, o_ref, w_ref):
    j = pl.program_id(0)
    n_rope_blocks = 2 * DIFF_W // MM_TN
    n_q_blocks = DIFF_W // MM_TN
    _cast_weights_once(w_f32_ref, w_ref)

    @pl.when(j < n_rope_blocks)
    def _():
        scale = jnp.where(j < n_q_blocks, LOG2_E * DIFF_QK_DIM ** -0.5, 1.0).astype(F32)

        def rope(rows, acc):
            cos = cos_ref[rows, :] * scale
            sin_lo = sin_lo_ref[rows, :] * scale
            sin_hi = sin_hi_ref[rows, :] * scale
            for c in range(MM_TN // LANES):
                xs = acc[:, c * LANES:(c + 1) * LANES]
                r = (xs * cos
                     + pltpu.roll(xs, LANES - DIFF_QK_DIM // 2, 1) * sin_lo
                     + pltpu.roll(xs, DIFF_QK_DIM // 2, 1) * sin_hi)
                o_ref[rows, c * LANES:(c + 1) * LANES] = r.astype(o_ref.dtype)

        _matmul_row_blocks(h_ref, w_ref, rope)

    @pl.when(j >= n_rope_blocks)
    def _():
        is_na_q = (j >= 3 * DIFF_W // MM_TN) & (j < (3 * DIFF_W + NA_W) // MM_TN)
        scale = jnp.where(is_na_q, LOG2_E * NA_HEAD_DIM ** -0.5, 1.0).astype(F32)

        def scaled(rows, acc):
            o_ref[rows, :] = (acc * scale).astype(o_ref.dtype)

        _matmul_row_blocks(h_ref, w_ref, scaled)


def _mlp_in_kernel(h_ref, w_f32_ref, o_ref, w_ref):
    _cast_weights_once(w_f32_ref, w_ref)

    def relu2(rows, acc):
        u = jnp.maximum(acc, 0.0)
        o_ref[rows, :] = (u * u).astype(o_ref.dtype)

    _matmul_row_blocks(h_ref, w_ref, relu2)


def _projection(kernel_fn, h, w, layer, tm, tn, extra=(), name=None):
    s, d = h.shape
    n = w.shape[2]
    tm = min(tm, s)
    extra_specs = [pl.BlockSpec((tm, LANES), lambda j, i: (i, 0)) for _ in extra]
    return pl.pallas_call(
        kernel_fn,
        grid=(n // tn, s // tm),
        in_specs=[pl.BlockSpec((tm, d), lambda j, i: (i, 0)),
                  pl.BlockSpec((None, d, tn), lambda j, i: (layer, 0, j))] + extra_specs,
        out_specs=pl.BlockSpec((tm, tn), lambda j, i: (i, j)),
        out_shape=jax.ShapeDtypeStruct((s, n), BF16),
        scratch_shapes=[pltpu.VMEM((d, tn), BF16)],
        compiler_params=_compiler_params(("arbitrary", "arbitrary")),
        name=name,
    )(h, w, *extra)


def _diff_attn_kernel(lam_ref, q_ref, k_ref, v_ref, lq1_ref, lk1_ref, lq2_ref, lk2_ref,
                      g_ref, o_ref, vt_ref, qpad_ref, s_ref, acc_ref,
                      *, tq, tk, n_chunks, n_blocks):
    for c in range(n_chunks):
        vt_ref[c] = v_ref[c * tk:(c + 1) * tk, :].astype(F32).T.astype(BF16)

    lookahead = ATT_LOOKAHEAD
    lambda_init = lam_ref[0]
    lam = (jnp.exp(jnp.sum(lq1_ref[...] * lk1_ref[...], axis=-1, keepdims=True))
           - jnp.exp(jnp.sum(lq2_ref[...] * lk2_ref[...], axis=-1, keepdims=True))
           + lambda_init)

    def load_queries(i):
        qt = q_ref[pl.ds(pl.multiple_of(i * tq, tq), tq), :].astype(F32).T
        row = lax.broadcasted_iota(jnp.int32, qt.shape, 0)
        qpad_ref[...] = jnp.concatenate([jnp.where(row < DIFF_QK_DIM, qt, 0.0),
                                         jnp.where(row >= DIFF_QK_DIM, qt, 0.0)],
                                        axis=1).astype(BF16)

    def scores(c):
        s = jnp.dot(k_ref[c * tk:(c + 1) * tk, :], qpad_ref[...],
                    preferred_element_type=F32)
        s_ref[c % ATT_SCORE_BUFFERS] = s
        return jnp.max(s, axis=0, keepdims=True)

    def softmax_chunk(c, cmax, m, l):
        m_new = jnp.maximum(m, cmax)
        alpha = jnp.exp2(m - m_new)
        p = jnp.exp2(s_ref[c % ATT_SCORE_BUFFERS] - m_new)
        l = alpha * l + jnp.sum(p, axis=0, keepdims=True)
        return m_new, l, alpha, p.astype(BF16)

    def accumulate(c, alpha, p):
        pv = jnp.dot(vt_ref[c], p, preferred_element_type=F32)
        acc_ref[...] = acc_ref[...] * alpha + pv

    def start_block(i):
        load_queries(i)
        return tuple(scores(c) for c in range(lookahead))

    def block(i, first_cmax):
        acc_ref[...] = jnp.zeros_like(acc_ref)
        m = jnp.full((1, 2 * tq), MASK_VALUE, F32)
        l = jnp.zeros((1, 2 * tq), F32)
        cmax = dict(enumerate(first_cmax))
        for c in range(n_chunks):
            if c + lookahead < n_chunks:
                cmax[c + lookahead] = scores(c + lookahead)
            m, l, alpha, p = softmax_chunk(c, cmax.pop(c), m, l)
            accumulate(c, alpha, p)
        next_cmax = start_block(jnp.minimum(i + 1, n_blocks - 1))

        o = acc_ref[...] / l
        od = (o[:, :tq] - lam * o[:, tq:]).T
        y = _rmsnorm_rows(od, g_ref[...]) * (1.0 - lambda_init)
        o_ref[pl.ds(pl.multiple_of(i * tq, tq), tq), :] = y.astype(o_ref.dtype)
        return next_cmax

    lax.fori_loop(0, n_blocks, block, start_block(0))


def _diff_attention(proj, lam_init, lq1, lk1, lq2, lk2, subln_g):
    s = proj.shape[0]
    tq = min(ATT_TQ, s)
    tk = min(ATT_TK, s)
    n_chunks = s // tk
    assert n_chunks % ATT_SCORE_BUFFERS == 0, "score slots must line up across query blocks"
    assert ATT_SCORE_BUFFERS >= 2 * ATT_LOOKAHEAD
    kernel_fn = functools.partial(_diff_attn_kernel, tq=tq, tk=tk, n_chunks=n_chunks,
                                  n_blocks=s // tq)
    k_col0 = DIFF_W // LANES
    v_col0 = 2 * DIFF_W // LANES
    vec = lambda n: pl.BlockSpec((1, n), lambda h: (0, 0))
    return pl.pallas_call(
        kernel_fn,
        grid=(N_DIFF_HEADS,),
        in_specs=[pl.BlockSpec(memory_space=pltpu.SMEM),
                  pl.BlockSpec((s, LANES), lambda h: (0, h)),
                  pl.BlockSpec((s, LANES), lambda h: (0, k_col0 + h)),
                  pl.BlockSpec((s, LANES), lambda h: (0, v_col0 + h)),
                  vec(DIFF_QK_DIM), vec(DIFF_QK_DIM), vec(DIFF_QK_DIM), vec(DIFF_QK_DIM),
                  vec(DIFF_V_DIM)],
        out_specs=pl.BlockSpec((s, LANES), lambda h: (0, h)),
        out_shape=jax.ShapeDtypeStruct((s, DIFF_W), BF16),
        scratch_shapes=[pltpu.VMEM((n_chunks, DIFF_V_DIM, tk), BF16),
                        pltpu.VMEM((LANES, 2 * tq), BF16),
                        pltpu.VMEM((ATT_SCORE_BUFFERS, tk, 2 * tq), F32),
                        pltpu.VMEM((DIFF_V_DIM, 2 * tq), F32)],
        compiler_params=_compiler_params(("arbitrary",)),
        name="diff_attention",
    )(lam_init, proj, proj, proj,
      lq1.reshape(1, -1), lk1.reshape(1, -1), lq2.reshape(1, -1), lk2.reshape(1, -1),
      subln_g.reshape(1, -1))


NA_GROUP = 4
NA_SPAN = 3 * NA_GROUP
NA_LOOKAHEAD = 2


def _na_span_start(g, n_groups):
    return jnp.clip(g - 1, 0, n_groups - 3)


def _na_attn_kernel(plan_ref, q_ref, k0_ref, k1_ref, k2_ref, v0_ref, v1_ref, v2_ref, bias_ref,
                    o_ref, kwin_ref, vwin_ref, *, n_groups):
    blk = NA_GROUP * GRID_W
    for t, (k_ref, v_ref) in enumerate(((k0_ref, v0_ref), (k1_ref, v1_ref), (k2_ref, v2_ref))):
        kwin_ref[t * blk:(t + 1) * blk, :] = k_ref[...]
        vwin_ref[t * blk:(t + 1) * blk, :] = v_ref[...]

    g = pl.program_id(0)
    variant = jnp.where(g == 0, 0, jnp.where(g == n_groups - 1, 2, 1))
    pairs = NA_SPAN // 2
    lane_half = lax.broadcasted_iota(jnp.int32, (1, 2 * GRID_W), 1) // GRID_W
    starts, row_masks = [], []
    for j in range(NA_GROUP):
        base = (variant * NA_GROUP + j) * 3
        starts.append(plan_ref[base])
        lo, hi = plan_ref[base + 1], plan_ref[base + 2]
        u = [2 * t + lane_half for t in range(pairs)]
        row_masks.append([(ut >= lo) & (ut < hi) for ut in u])

    def bias_rows(h, j):
        tiles = [jnp.where(row_masks[j][t], bias_ref[h, starts[j] + 2 * t], MASK_VALUE)
                 for t in range(pairs)]
        return jnp.concatenate(tiles, axis=1)

    def head_cols(h):
        return slice(h * NA_HEAD_DIM, (h + 1) * NA_HEAD_DIM)

    def scores(h):
        sc = lax.dot_general(q_ref[:, head_cols(h)], kwin_ref[:, head_cols(h)],
                             (((1,), (1,)), ((), ())),
                             preferred_element_type=F32)
        return sc + jnp.concatenate([bias_rows(h, j) for j in range(NA_GROUP)], axis=0)

    pending = {h: scores(h) for h in range(NA_LOOKAHEAD)}
    for h in range(N_NA_HEADS):
        if h + NA_LOOKAHEAD < N_NA_HEADS:
            pending[h + NA_LOOKAHEAD] = scores(h + NA_LOOKAHEAD)
        sc = pending.pop(h)
        p = jnp.exp2(sc - jnp.max(sc, axis=-1, keepdims=True))
        l = jnp.sum(p, axis=-1, keepdims=True)
        pv = jnp.dot(p.astype(BF16), vwin_ref[:, head_cols(h)], preferred_element_type=F32)
        o_ref[:, head_cols(h)] = pv / l


def _na_window_plan(rows):
    n_groups = rows // NA_GROUP
    kr = NA_WIN_ROWS

    def plan(g):
        start = NA_GROUP * int(np.clip(g - 1, 0, n_groups - 3))
        out = np.full((NA_GROUP, NA_SPAN), -1, np.int64)
        for j in range(NA_GROUP):
            r = NA_GROUP * g + j
            rs = int(np.clip(r - kr // 2, 0, rows - kr))
            for u in range(NA_SPAN):
                if rs <= start + u < rs + kr:
                    out[j, u] = start + u - r + NA_WIN_ROWS - 1
        assert (out >= 0).sum() == NA_GROUP * kr
        return out

    plans = [plan(0), plan(1), plan(n_groups - 1)]
    for g in range(1, n_groups - 1):
        assert (plan(g) == plans[1]).all()
    return plans


def _na_row_windows(rows):
    out = []
    for plan in _na_window_plan(rows):
        for j in range(NA_GROUP):
            valid_u = np.nonzero(plan[j] >= 0)[0]
            assert (np.diff(valid_u) == 1).all() and (np.diff(plan[j, valid_u]) == 1).all()
            out.append((int(plan[j, valid_u[0]] - valid_u[0]), int(valid_u[0]),
                        int(valid_u[-1]) + 1))
    return out


def _na_bias_table(rpb, rows):
    c = np.arange(GRID_W)
    cs = np.clip(c - NA_WIN_COLS // 2, 0, GRID_W - NA_WIN_COLS)
    kc = np.arange(GRID_W)
    col_valid = (kc[None, :] >= cs[:, None]) & (kc[None, :] < cs[:, None] + NA_WIN_COLS)
    pad = GRID_W - NA_WIN_COLS
    rp = jnp.pad(rpb.astype(F32) * LOG2_E, ((0, 0), (0, 0), (pad, pad)))
    by_col = jnp.stack([lax.slice_in_dim(rp, NA_WIN_COLS - 1 - ci + pad,
                                         NA_WIN_COLS - 1 - ci + pad + GRID_W, axis=2)
                        for ci in range(GRID_W)], axis=2)
    by_col = jnp.where(col_valid[None, None], by_col, MASK_VALUE)
    windows = _na_row_windows(rows)
    front = max(0, -min(w[0] for w in windows))
    back = max(0, max(w[0] for w in windows) + NA_SPAN - by_col.shape[1])
    padded = jnp.pad(by_col, ((0, 0), (front, back + 1), (0, 0), (0, 0)))
    table = jnp.concatenate([padded[:, :-1], padded[:, 1:]], axis=-1)
    plan = np.array([[w[0] + front, w[1], w[2]] for w in windows], np.int32).reshape(-1)
    return table, jnp.asarray(plan)


def _na_attention(proj, bias, plan, layer):
    s = proj.shape[0]
    rows = s // GRID_W
    assert rows % NA_GROUP == 0 and rows >= NA_SPAN and NA_SPAN >= NA_GROUP + NA_WIN_ROWS - 1
    assert NA_SPAN % 2 == 0
    n_groups = rows // NA_GROUP
    blk = NA_GROUP * GRID_W
    q_col = 3 * DIFF_W // NA_W
    k_col, v_col = q_col + 1, q_col + 2
    bias = bias.reshape((-1, N_NA_HEADS) + bias.shape[1:])

    def kv_spec(t, col):
        return pl.BlockSpec((blk, NA_W),
                            lambda g, plan_ref: (_na_span_start(g, n_groups) + t, col))

    grid_spec = pltpu.PrefetchScalarGridSpec(
        num_scalar_prefetch=1,
        grid=(n_groups,),
        in_specs=[pl.BlockSpec((blk, NA_W), lambda g, plan_ref: (g, q_col))]
                 + [kv_spec(t, k_col) for t in range(3)]
                 + [kv_spec(t, v_col) for t in range(3)]
                 + [pl.BlockSpec((None,) + bias.shape[1:],
                                 lambda g, plan_ref: (layer, 0, 0, 0, 0))],
        out_specs=pl.BlockSpec((blk, NA_W), lambda g, plan_ref: (g, 0)),
        scratch_shapes=[pltpu.VMEM((NA_SPAN * GRID_W, NA_W), BF16),
                        pltpu.VMEM((NA_SPAN * GRID_W, NA_W), BF16)])
    return pl.pallas_call(
        functools.partial(_na_attn_kernel, n_groups=n_groups),
        grid_spec=grid_spec,
        out_shape=jax.ShapeDtypeStruct((s, NA_W), F32),
        compiler_params=_compiler_params(("arbitrary",)),
        name="na_attention",
    )(plan, *([proj] * 7), bias)


def _out_proj_kernel(a_ref, b_ref, na_g_ref, w_ref, x_ref, mlp_g_ref, o_ref, h_ref):
    ka = a_ref.shape[1]
    tr = a_ref.shape[0] // OUT_PROJ_ROW_SPLIT
    row_slices = [slice(r * tr, (r + 1) * tr) for r in range(OUT_PROJ_ROW_SPLIT)]

    def normed_b(r):
        return _rmsnorm_rows(b_ref[row_slices[r], :], na_g_ref[...]).astype(BF16)

    b = normed_b(0)
    for r, rows in enumerate(row_slices):
        b_next = normed_b(r + 1) if r + 1 < OUT_PROJ_ROW_SPLIT else None
        acc = jnp.dot(a_ref[rows, :], w_ref[:ka, :], preferred_element_type=F32)
        acc += jnp.dot(b, w_ref[ka:, :], preferred_element_type=F32)
        x_new = x_ref[rows, :] + acc
        o_ref[rows, :] = x_new
        h_ref[rows, :] = _rmsnorm_rows(x_new, mlp_g_ref[...]).astype(h_ref.dtype)
        b = b_next


def _out_proj(o_diff, o_na, na_g, w, layer, x, mlp_g):
    s, d = x.shape
    tm = min(OUT_PROJ_TM, s)
    ka, kb = o_diff.shape[1], o_na.shape[1]
    row_block = lambda width: pl.BlockSpec((tm, width), lambda i: (i, 0))
    vec = lambda width: pl.BlockSpec((1, width), lambda i: (0, 0))
    return pl.pallas_call(
        _out_proj_kernel,
        grid=(s // tm,),
        in_specs=[row_block(ka), row_block(kb), vec(kb),
                  pl.BlockSpec((None, ka + kb, d), lambda i: (layer, 0, 0)),
                  row_block(d), vec(d)],
        out_specs=(row_block(d), row_block(d)),
        out_shape=(jax.ShapeDtypeStruct((s, d), F32), jax.ShapeDtypeStruct((s, d), BF16)),
        compiler_params=_compiler_params(("arbitrary",)),
        name="out_proj",
    )(o_diff, o_na, na_g.reshape(1, kb), w, x, mlp_g.reshape(1, d))


def _mlp_out_kernel(u_ref, w_ref, x_ref, g_ref, o_ref, *maybe_h_ref):
    k = pl.program_id(1)

    @pl.when(k == 0)
    def _():
        o_ref[...] = x_ref[...]

    o_ref[...] += jnp.dot(u_ref[...], w_ref[...], preferred_element_type=F32)

    @pl.when(k == pl.num_programs(1) - 1)
    def _():
        normed = _rmsnorm_rows(o_ref[...], g_ref[...])
        if maybe_h_ref:
            maybe_h_ref[0][...] = normed.astype(maybe_h_ref[0].dtype)
        else:
            o_ref[...] = normed


def _mlp_out(u, w, layer, x, g, is_last_layer):
    s, d = x.shape
    f = u.shape[1]
    tm = min(MM_TM, s)
    row_block = pl.BlockSpec((tm, d), lambda i, k: (i, 0))
    x_shape = jax.ShapeDtypeStruct((s, d), F32)
    return pl.pallas_call(
        _mlp_out_kernel,
        grid=(s // tm, f // MLP_OUT_TK),
        in_specs=[pl.BlockSpec((tm, MLP_OUT_TK), lambda i, k: (i, k)),
                  pl.BlockSpec((None, MLP_OUT_TK, d), lambda i, k: (layer, k, 0)),
                  row_block, pl.BlockSpec((1, d), lambda i, k: (0, 0))],
        out_specs=row_block if is_last_layer else (row_block, row_block),
        out_shape=x_shape if is_last_layer else (x_shape, jax.ShapeDtypeStruct((s, d), BF16)),
        compiler_params=_compiler_params(("arbitrary", "arbitrary")),
        name="mlp_out_final" if is_last_layer else "mlp_out",
    )(u, w, x, g.reshape(1, d))


def _rope_tables(seq):
    inv_freq = 1.0 / (ROPE_THETA ** (jnp.arange(0, DIFF_QK_DIM, 2, dtype=F32) / DIFF_QK_DIM))
    ang = jnp.arange(seq, dtype=F32)[:, None] * inv_freq[None, :]
    cos, sin = jnp.cos(ang), jnp.sin(ang)
    zero = jnp.zeros_like(sin)
    reps = LANES // DIFF_QK_DIM
    cos_t = jnp.tile(cos, (1, 2 * reps))
    sin_lo = jnp.tile(jnp.concatenate([-sin, zero], axis=1), (1, reps))
    sin_hi = jnp.tile(jnp.concatenate([zero, sin], axis=1), (1, reps))
    return cos_t, sin_lo, sin_hi


def kernel(x, attn_norm, w_in, lambda_q1, lambda_k1, lambda_q2, lambda_k2, diff_subln, na_norm,
           na_rpb, w_out, mlp_norm, w_mlp_in, w_mlp_out, final_norm):
    b, s, d = x.shape
    depth = w_in.shape[0]
    rope = _rope_tables(s)
    na_bias, na_plan = _na_bias_table(
        na_rpb.reshape((depth * N_NA_HEADS,) + na_rpb.shape[2:]), s // GRID_W)
    w_out, w_mlp_out = w_out.astype(BF16), w_mlp_out.astype(BF16)
    w_in, w_mlp_in = w_in.astype(F32), w_mlp_in.astype(F32)
    outs = []
    for bi in range(b):
        xb = x[bi].astype(F32)
        h_attn = _rmsnorm_cast(xb, attn_norm[0].astype(F32))
        for l in range(depth):
            lambda_init = 0.8 - 0.6 * math.exp(-0.3 * l)
            proj = _projection(_in_proj_kernel, h_attn, w_in, l, MM_TM, MM_TN, rope,
                               name="in_proj")
            o_diff = _diff_attention(proj, jnp.full((1,), lambda_init, F32),
                                     lambda_q1[l].astype(F32), lambda_k1[l].astype(F32),
                                     lambda_q2[l].astype(F32), lambda_k2[l].astype(F32),
                                     diff_subln[l].astype(F32))
            o_na = _na_attention(proj, na_bias, na_plan, l)
            xb, h_mlp = _out_proj(o_diff, o_na, na_norm[l].astype(F32), w_out, l, xb,
                                  mlp_norm[l].astype(F32))
            u = _projection(_mlp_in_kernel, h_mlp, w_mlp_in, l, MLP_IN_TM, MLP_IN_TN,
                            name="mlp_in")
            if l == depth - 1:
                xb = _mlp_out(u, w_mlp_out, l, xb, final_norm.astype(F32), True)
            else:
                xb, h_attn = _mlp_out(u, w_mlp_out, l, xb, attn_norm[l + 1].astype(F32), False)
        outs.append(xb)
    out = outs[0][None] if b == 1 else jnp.stack(outs)
    return out.astype(x.dtype)
```

```python
import functools
import math

import jax
import jax.numpy as jnp
import numpy as np
from jax import lax
from jax.experimental import pallas as pl
from jax.experimental.pallas import tpu as pltpu

F32 = jnp.float32
BF16 = jnp.bfloat16

GRID_W = 64
N_DIFF_HEADS = 8
DIFF_QK_DIM = 64
DIFF_V_DIM = 128
DIFF_W = N_DIFF_HEADS * DIFF_V_DIM
N_NA_HEADS = 8
NA_HEAD_DIM = 128
NA_W = N_NA_HEADS * NA_HEAD_DIM
NA_WIN_ROWS = 8
NA_WIN_COLS = 16
ROPE_THETA = 10000.0
EPS = 1e-6
MASK_VALUE = -1e30
LOG2_E = math.log2(math.e)

LANES = 128
VMEM_LIMIT_BYTES = 56 * 1024 * 1024

MM_TM = 1024
MM_TN = 1024
IN_PROJ_TM = 2048
MLP_IN_TM = 2048
MLP_IN_TN = 1024
OUT_PROJ_TM = 512
OUT_PROJ_ROW_SPLIT = 2
MM_ROW_SPLIT = 4
MLP_OUT_TK = 1024
ATT_TQ = 256
ATT_TK = 512
ATT_LOOKAHEAD = 2
ATT_SCORE_BUFFERS = 4


def _compiler_params(semantics, flags=None):
    return pltpu.CompilerParams(dimension_semantics=semantics,
                                vmem_limit_bytes=VMEM_LIMIT_BYTES, flags=flags)


def _rmsnorm_rows(x, g):
    return x * lax.rsqrt(jnp.mean(x * x, axis=-1, keepdims=True) + EPS) * g


def _rmsnorm_cast_kernel(x_ref, g_ref, o_ref):
    o_ref[...] = _rmsnorm_rows(x_ref[...], g_ref[...]).astype(o_ref.dtype)


def _rmsnorm_cast(x, g):
    s, d = x.shape
    tm = min(MM_TM, s)
    return pl.pallas_call(
        _rmsnorm_cast_kernel,
        grid=(s // tm,),
        in_specs=[pl.BlockSpec((tm, d), lambda i: (i, 0)), pl.BlockSpec((1, d), lambda i: (0, 0))],
        out_specs=pl.BlockSpec((tm, d), lambda i: (i, 0)),
        out_shape=jax.ShapeDtypeStruct((s, d), BF16),
        compiler_params=_compiler_params(("arbitrary",)),
        name="first_norm",
    )(x, g.reshape(1, d))


def _cast_weights_once(w_ref, w_bf16_ref):
    @pl.when(pl.program_id(1) == 0)
    def _():
        w_bf16_ref[...] = w_ref[...].astype(BF16)


def _matmul_row_blocks(h_ref, w_ref, epilogue):
    tr = h_ref.shape[0] // MM_ROW_SPLIT
    for r in range(MM_ROW_SPLIT):
        rows = slice(r * tr, (r + 1) * tr)
        epilogue(rows, jnp.dot(h_ref[rows, :], w_ref[...], preferred_element_type=F32))


def _in_proj_kernel(h_ref, w_f32_ref, cos_ref, sin_lo_ref, sin_hi_ref, o_ref, w_ref):
    j = pl.program_id(0)
    n_rope_blocks = 2 * DIFF_W // MM_TN
    n_q_blocks = DIFF_W // MM_TN
    _cast_weights_once(w_f32_ref, w_ref)

    @pl.when(j < n_rope_blocks)
    def _():
        scale = jnp.where(j < n_q_blocks, LOG2_E * DIFF_QK_DIM ** -0.5, 1.0).astype(F32)

        def rope(rows, acc):
            cos = cos_ref[rows, :] * scale
            sin_lo = sin_lo_ref[rows, :] * scale
            sin_hi = sin_hi_ref[rows, :] * scale
            for c in range(MM_TN // LANES):
                xs = acc[:, c * LANES:(c + 1) * LANES]
                r = (xs * cos
                     + pltpu.roll(xs, LANES - DIFF_QK_DIM // 2, 1) * sin_lo
                     + pltpu.roll(xs, DIFF_QK_DIM // 2, 1) * sin_hi)
                o_ref[rows, c * LANES:(c + 1) * LANES] = r.astype(o_ref.dtype)

        _matmul_row_blocks(h_ref, w_ref, rope)

    @pl.when(j >= n_rope_blocks)
    def _():
        is_na_q = (j >= 3 * DIFF_W // MM_TN) & (j < (3 * DIFF_W + NA_W) // MM_TN)
        scale = jnp.where(is_na_q, LOG2_E * NA_HEAD_DIM ** -0.5, 1.0).astype(F32)

        def scaled(rows, acc):
            o_ref[rows, :] = (acc * scale).astype(o_ref.dtype)

        _matmul_row_blocks(h_ref, w_ref, scaled)


def _mlp_in_kernel(h_ref, w_f32_ref, o_ref, w_ref):
    _cast_weights_once(w_f32_ref, w_ref)

    def relu2(rows, acc):
        u = jnp.maximum(acc, 0.0)
        o_ref[rows, :] = (u * u).astype(o_ref.dtype)

    _matmul_row_blocks(h_ref, w_ref, relu2)


def _projection(kernel_fn, h, w, layer, tm, tn, extra=(), name=None):
    s, d = h.shape
    n = w.shape[2]
    tm = min(tm, s)
    extra_specs = [pl.BlockSpec((tm, LANES), lambda j, i: (i, 0)) for _ in extra]
    return pl.pallas_call(
        kernel_fn,
        grid=(n // tn, s // tm),
        in_specs=[pl.BlockSpec((tm, d), lambda j, i: (i, 0)),
                  pl.BlockSpec((None, d, tn), lambda j, i: (layer, 0, j))] + extra_specs,
        out_specs=pl.BlockSpec((tm, tn), lambda j, i: (i, j)),
        out_shape=jax.ShapeDtypeStruct((s, n), BF16),
        scratch_shapes=[pltpu.VMEM((d, tn), BF16)],
        compiler_params=_compiler_params(("arbitrary", "arbitrary")),
        name=name,
    )(h, w, *extra)


def _diff_attn_kernel(lam_ref, q_ref, k_ref, v_ref, lq1_ref, lk1_ref, lq2_ref, lk2_ref,
                      g_ref, o_ref, vt_ref, qpad_ref, s_ref, acc_ref,
                      *, tq, tk, n_chunks, n_blocks):
    for c in range(n_chunks):
        vt_ref[c] = v_ref[c * tk:(c + 1) * tk, :].astype(F32).T.astype(BF16)

    lookahead = ATT_LOOKAHEAD
    lambda_init = lam_ref[0]
    lam = (jnp.exp(jnp.sum(lq1_ref[...] * lk1_ref[...], axis=-1, keepdims=True))
           - jnp.exp(jnp.sum(lq2_ref[...] * lk2_ref[...], axis=-1, keepdims=True))
           + lambda_init)

    def load_queries(i):
        qt = q_ref[pl.ds(pl.multiple_of(i * tq, tq), tq), :].astype(F32).T
        row = lax.broadcasted_iota(jnp.int32, qt.shape, 0)
        qpad_ref[...] = jnp.concatenate([jnp.where(row < DIFF_QK_DIM, qt, 0.0),
                                         jnp.where(row >= DIFF_QK_DIM, qt, 0.0)],
                                        axis=1).astype(BF16)

    def scores(c):
        s = jnp.dot(k_ref[c * tk:(c + 1) * tk, :], qpad_ref[...],
                    preferred_element_type=F32)
        s_ref[c % ATT_SCORE_BUFFERS] = s
        return jnp.max(s, axis=0, keepdims=True)

    def softmax_chunk(c, cmax, m, l):
        m_new = jnp.maximum(m, cmax)
        alpha = jnp.exp2(m - m_new)
        p = jnp.exp2(s_ref[c % ATT_SCORE_BUFFERS] - m_new)
        l = alpha * l + jnp.sum(p, axis=0, keepdims=True)
        return m_new, l, alpha, p.astype(BF16)

    def accumulate(c, alpha, p):
        pv = jnp.dot(vt_ref[c], p, preferred_element_type=F32)
        acc_ref[...] = acc_ref[...] * alpha + pv

    def start_block(i):
        load_queries(i)
        return tuple(scores(c) for c in range(lookahead))

    def block(i, first_cmax):
        acc_ref[...] = jnp.zeros_like(acc_ref)
        m = jnp.full((1, 2 * tq), MASK_VALUE, F32)
        l = jnp.zeros((1, 2 * tq), F32)
        cmax = dict(enumerate(first_cmax))
        for c in range(n_chunks):
            if c + lookahead < n_chunks:
                cmax[c + lookahead] = scores(c + lookahead)
            m, l, alpha, p = softmax_chunk(c, cmax.pop(c), m, l)
            accumulate(c, alpha, p)
        next_cmax = start_block(jnp.minimum(i + 1, n_blocks - 1))

        o = acc_ref[...] / l
        od = (o[:, :tq] - lam * o[:, tq:]).T
        y = _rmsnorm_rows(od, g_ref[...]) * (1.0 - lambda_init)
        o_ref[pl.ds(pl.multiple_of(i * tq, tq), tq), :] = y.astype(o_ref.dtype)
        return next_cmax

    lax.fori_loop(0, n_blocks, block, start_block(0))


def _diff_attention(proj, lam_init, lq1, lk1, lq2, lk2, subln_g):
    s = proj.shape[0]
    tq = min(ATT_TQ, s)
    tk = min(ATT_TK, s)
    n_chunks = s // tk
    assert n_chunks % ATT_SCORE_BUFFERS == 0, "score slots must line up across query blocks"
    assert ATT_SCORE_BUFFERS >= 2 * ATT_LOOKAHEAD
    kernel_fn = functools.partial(_diff_attn_kernel, tq=tq, tk=tk, n_chunks=n_chunks,
                                  n_blocks=s // tq)
    k_col0 = DIFF_W // LANES
    v_col0 = 2 * DIFF_W // LANES
    vec = lambda n: pl.BlockSpec((1, n), lambda h: (0, 0))
    return pl.pallas_call(
        kernel_fn,
        grid=(N_DIFF_HEADS,),
        in_specs=[pl.BlockSpec(memory_space=pltpu.SMEM),
                  pl.BlockSpec((s, LANES), lambda h: (0, h)),
                  pl.BlockSpec((s, LANES), lambda h: (0, k_col0 + h)),
                  pl.BlockSpec((s, LANES), lambda h: (0, v_col0 + h)),
                  vec(DIFF_QK_DIM), vec(DIFF_QK_DIM), vec(DIFF_QK_DIM), vec(DIFF_QK_DIM),
                  vec(DIFF_V_DIM)],
        out_specs=pl.BlockSpec((s, LANES), lambda h: (0, h)),
        out_shape=jax.ShapeDtypeStruct((s, DIFF_W), BF16),
        scratch_shapes=[pltpu.VMEM((n_chunks, DIFF_V_DIM, tk), BF16),
                        pltpu.VMEM((LANES, 2 * tq), BF16),
                        pltpu.VMEM((ATT_SCORE_BUFFERS, tk, 2 * tq), F32),
                        pltpu.VMEM((DIFF_V_DIM, 2 * tq), F32)],
        compiler_params=_compiler_params(("arbitrary",)),
        name="diff_attention",
    )(lam_init, proj, proj, proj,
      lq1.reshape(1, -1), lk1.reshape(1, -1), lq2.reshape(1, -1), lk2.reshape(1, -1),
      subln_g.reshape(1, -1))


NA_GROUP = 4
NA_SPAN = 3 * NA_GROUP
NA_LOOKAHEAD = 2


def _na_span_start(g, n_groups):
    return jnp.clip(g - 1, 0, n_groups - 3)


def _na_attn_kernel(plan_ref, q_ref, k0_ref, k1_ref, k2_ref, v0_ref, v1_ref, v2_ref, bias_ref,
                    o_ref, kwin_ref, vwin_ref, *, n_groups):
    blk = NA_GROUP * GRID_W
    for t, (k_ref, v_ref) in enumerate(((k0_ref, v0_ref), (k1_ref, v1_ref), (k2_ref, v2_ref))):
        kwin_ref[t * blk:(t + 1) * blk, :] = k_ref[...]
        vwin_ref[t * blk:(t + 1) * blk, :] = v_ref[...]

    g = pl.program_id(0)
    variant = jnp.where(g == 0, 0, jnp.where(g == n_groups - 1, 2, 1))
    pairs = NA_SPAN // 2
    lane_half = lax.broadcasted_iota(jnp.int32, (1, 2 * GRID_W), 1) // GRID_W
    starts, row_masks = [], []
    for j in range(NA_GROUP):
        base = (variant * NA_GROUP + j) * 3
        starts.append(plan_ref[base])
        lo, hi = plan_ref[base + 1], plan_ref[base + 2]
        u = [2 * t + lane_half for t in range(pairs)]
        row_masks.append([(ut >= lo) & (ut < hi) for ut in u])

    def bias_rows(h, j):
        tiles = [jnp.where(row_masks[j][t], bias_ref[h, starts[j] + 2 * t], MASK_VALUE)
                 for t in range(pairs)]
        return jnp.concatenate(tiles, axis=1)

    def head_cols(h):
        return slice(h * NA_HEAD_DIM, (h + 1) * NA_HEAD_DIM)

    def scores(h):
        sc = lax.dot_general(q_ref[:, head_cols(h)], kwin_ref[:, head_cols(h)],
                             (((1,), (1,)), ((), ())),
                             preferred_element_type=F32)
        return sc + jnp.concatenate([bias_rows(h, j) for j in range(NA_GROUP)], axis=0)

    pending = {h: scores(h) for h in range(NA_LOOKAHEAD)}
    for h in range(N_NA_HEADS):
        if h + NA_LOOKAHEAD < N_NA_HEADS:
            pending[h + NA_LOOKAHEAD] = scores(h + NA_LOOKAHEAD)
        sc = pending.pop(h)
        p = jnp.exp2(sc - jnp.max(sc, axis=-1, keepdims=True))
        l = jnp.sum(p, axis=-1, keepdims=True)
        pv = jnp.dot(p.astype(BF16), vwin_ref[:, head_cols(h)], preferred_element_type=F32)
        o_ref[:, head_cols(h)] = pv / l


def _na_window_plan(rows):
    n_groups = rows // NA_GROUP
    kr = NA_WIN_ROWS

    def plan(g):
        start = NA_GROUP * int(np.clip(g - 1, 0, n_groups - 3))
        out = np.full((NA_GROUP, NA_SPAN), -1, np.int64)
        for j in range(NA_GROUP):
            r = NA_GROUP * g + j
            rs = int(np.clip(r - kr // 2, 0, rows - kr))
            for u in range(NA_SPAN):
                if rs <= start + u < rs + kr:
                    out[j, u] = start + u - r + NA_WIN_ROWS - 1
        assert (out >= 0).sum() == NA_GROUP * kr
        return out

    plans = [plan(0), plan(1), plan(n_groups - 1)]
    for g in range(1, n_groups - 1):
        assert (plan(g) == plans[1]).all()
    return plans


def _na_row_windows(rows):
    out = []
    for plan in _na_window_plan(rows):
        for j in range(NA_GROUP):
            valid_u = np.nonzero(plan[j] >= 0)[0]
            assert (np.diff(valid_u) == 1).all() and (np.diff(plan[j, valid_u]) == 1).all()
            out.append((int(plan[j, valid_u[0]] - valid_u[0]), int(valid_u[0]),
                        int(valid_u[-1]) + 1))
    return out


def _na_bias_table(rpb, rows):
    c = np.arange(GRID_W)
    cs = np.clip(c - NA_WIN_COLS // 2, 0, GRID_W - NA_WIN_COLS)
    kc = np.arange(GRID_W)
    col_valid = (kc[None, :] >= cs[:, None]) & (kc[None, :] < cs[:, None] + NA_WIN_COLS)
    pad = GRID_W - NA_WIN_COLS
    rp = jnp.pad(rpb.astype(F32) * LOG2_E, ((0, 0), (0, 0), (pad, pad)))
    by_col = jnp.stack([lax.slice_in_dim(rp, NA_WIN_COLS - 1 - ci + pad,
                                         NA_WIN_COLS - 1 - ci + pad + GRID_W, axis=2)
                        for ci in range(GRID_W)], axis=2)
    by_col = jnp.where(col_valid[None, None], by_col, MASK_VALUE)
    windows = _na_row_windows(rows)
    front = max(0, -min(w[0] for w in windows))
    back = max(0, max(w[0] for w in windows) + NA_SPAN - by_col.shape[1])
    padded = jnp.pad(by_col, ((0, 0), (front, back + 1), (0, 0), (0, 0)))
    table = jnp.concatenate([padded[:, :-1], padded[:, 1:]], axis=-1)
    plan = np.array([[w[0] + front, w[1], w[2]] for w in windows], np.int32).reshape(-1)
    return table, jnp.asarray(plan)


def _na_attention(proj, bias, plan, layer):
    s = proj.shape[0]
    rows = s // GRID_W
    assert rows % NA_GROUP == 0 and rows >= NA_SPAN and NA_SPAN >= NA_GROUP + NA_WIN_ROWS - 1
    assert NA_SPAN % 2 == 0
    n_groups = rows // NA_GROUP
    blk = NA_GROUP * GRID_W
    q_col = 3 * DIFF_W // NA_W
    k_col, v_col = q_col + 1, q_col + 2
    bias = bias.reshape((-1, N_NA_HEADS) + bias.shape[1:])

    def kv_spec(t, col):
        return pl.BlockSpec((blk, NA_W),
                            lambda g, plan_ref: (_na_span_start(g, n_groups) + t, col))

    grid_spec = pltpu.PrefetchScalarGridSpec(
        num_scalar_prefetch=1,
        grid=(n_groups,),
        in_specs=[pl.BlockSpec((blk, NA_W), lambda g, plan_ref: (g, q_col))]
                 + [kv_spec(t, k_col) for t in range(3)]
                 + [kv_spec(t, v_col) for t in range(3)]
                 + [pl.BlockSpec((None,) + bias.shape[1:],
                                 lambda g, plan_ref: (layer, 0, 0, 0, 0))],
        out_specs=pl.BlockSpec((blk, NA_W), lambda g, plan_ref: (g, 0)),
        scratch_shapes=[pltpu.VMEM((NA_SPAN * GRID_W, NA_W), BF16),
                        pltpu.VMEM((NA_SPAN * GRID_W, NA_W), BF16)])
    return pl.pallas_call(
        functools.partial(_na_attn_kernel, n_groups=n_groups),
        grid_spec=grid_spec,
        out_shape=jax.ShapeDtypeStruct((s, NA_W), F32),
        compiler_params=_compiler_params(("arbitrary",)),
        name="na_attention",
    )(plan, *([proj] * 7), bias)


def _out_proj_kernel(a_ref, b_ref, na_g_ref, w_ref, x_ref, mlp_g_ref, o_ref, h_ref):
    ka = a_ref.shape[1]
    tr = a_ref.shape[0] // OUT_PROJ_ROW_SPLIT
    row_slices = [slice(r * tr, (r + 1) * tr) for r in range(OUT_PROJ_ROW_SPLIT)]

    def normed_b(r):
        return _rmsnorm_rows(b_ref[row_slices[r], :], na_g_ref[...]).astype(BF16)

    b = normed_b(0)
    for r, rows in enumerate(row_slices):
        b_next = normed_b(r + 1) if r + 1 < OUT_PROJ_ROW_SPLIT else None
        acc = jnp.dot(a_ref[rows, :], w_ref[:ka, :], preferred_element_type=F32)
        acc += jnp.dot(b, w_ref[ka:, :], preferred_element_type=F32)
        x_new = x_ref[rows, :] + acc
        o_ref[rows, :] = x_new
        h_ref[rows, :] = _rmsnorm_rows(x_new, mlp_g_ref[...]).astype(h_ref.dtype)
        b = b_next


def _out_proj(o_diff, o_na, na_g, w, layer, x, mlp_g):
    s, d = x.shape
    tm = min(OUT_PROJ_TM, s)
    ka, kb = o_diff.shape[1], o_na.shape[1]
    row_block = lambda width: pl.BlockSpec((tm, width), lambda i: (i, 0))
    vec = lambda width: pl.BlockSpec((1, width), lambda i: (0, 0))
    return pl.pallas_call(
        _out_proj_kernel,
        grid=(s // tm,),
        in_specs=[row_block(ka), row_block(kb), vec(kb),
                  pl.BlockSpec((None, ka + kb, d), lambda i: (layer, 0, 0)),
                  row_block(d), vec(d)],
        out_specs=(row_block(d), row_block(d)),
        out_shape=(jax.ShapeDtypeStruct((s, d), F32), jax.ShapeDtypeStruct((s, d), BF16)),
        compiler_params=_compiler_params(("arbitrary",)),
        name="out_proj",
    )(o_diff, o_na, na_g.reshape(1, kb), w, x, mlp_g.reshape(1, d))


def _mlp_out_kernel(u_ref, w_ref, x_ref, g_ref, o_ref, *maybe_h_ref):
    k = pl.program_id(1)

    @pl.when(k == 0)
    def _():
        o_ref[...] = x_ref[...]

    is_last = k == pl.num_programs(1) - 1

    @pl.when(jnp.logical_not(is_last))
    def _():
        o_ref[...] += jnp.dot(u_ref[...], w_ref[...], preferred_element_type=F32)

    @pl.when(is_last)
    def _():
        tr = o_ref.shape[0] // MM_ROW_SPLIT
        for r in range(MM_ROW_SPLIT):
            rows = slice(r * tr, (r + 1) * tr)
            x_new = o_ref[rows, :] + jnp.dot(u_ref[rows, :], w_ref[...],
                                             preferred_element_type=F32)
            normed = _rmsnorm_rows(x_new, g_ref[...])
            if maybe_h_ref:
                o_ref[rows, :] = x_new
                maybe_h_ref[0][rows, :] = normed.astype(maybe_h_ref[0].dtype)
            else:
                o_ref[rows, :] = normed


def _mlp_out(u, w, layer, x, g, is_last_layer):
    s, d = x.shape
    f = u.shape[1]
    tm = min(MM_TM, s)
    row_block = pl.BlockSpec((tm, d), lambda i, k: (i, 0))
    x_shape = jax.ShapeDtypeStruct((s, d), F32)
    return pl.pallas_call(
        _mlp_out_kernel,
        grid=(s // tm, f // MLP_OUT_TK),
        in_specs=[pl.BlockSpec((tm, MLP_OUT_TK), lambda i, k: (i, k)),
                  pl.BlockSpec((None, MLP_OUT_TK, d), lambda i, k: (layer, k, 0)),
                  row_block, pl.BlockSpec((1, d), lambda i, k: (0, 0))],
        out_specs=row_block if is_last_layer else (row_block, row_block),
        out_shape=x_shape if is_last_layer else (x_shape, jax.ShapeDtypeStruct((s, d), BF16)),
        compiler_params=_compiler_params(("arbitrary", "arbitrary")),
        name="mlp_out_final" if is_last_layer else "mlp_out",
    )(u, w, x, g.reshape(1, d))


def _rope_tables(seq):
    inv_freq = 1.0 / (ROPE_THETA ** (jnp.arange(0, DIFF_QK_DIM, 2, dtype=F32) / DIFF_QK_DIM))
    ang = jnp.arange(seq, dtype=F32)[:, None] * inv_freq[None, :]
    cos, sin = jnp.cos(ang), jnp.sin(ang)
    zero = jnp.zeros_like(sin)
    reps = LANES // DIFF_QK_DIM
    cos_t = jnp.tile(cos, (1, 2 * reps))
    sin_lo = jnp.tile(jnp.concatenate([-sin, zero], axis=1), (1, reps))
    sin_hi = jnp.tile(jnp.concatenate([zero, sin], axis=1), (1, reps))
    return cos_t, sin_lo, sin_hi


def kernel(x, attn_norm, w_in, lambda_q1, lambda_k1, lambda_q2, lambda_k2, diff_subln, na_norm,
           na_rpb, w_out, mlp_norm, w_mlp_in, w_mlp_out, final_norm):
    b, s, d = x.shape
    depth = w_in.shape[0]
    rope = _rope_tables(s)
    na_bias, na_plan = _na_bias_table(
        na_rpb.reshape((depth * N_NA_HEADS,) + na_rpb.shape[2:]), s // GRID_W)
    w_out, w_mlp_out = w_out.astype(BF16), w_mlp_out.astype(BF16)
    w_in, w_mlp_in = w_in.astype(F32), w_mlp_in.astype(F32)
    outs = []
    for bi in range(b):
        xb = x[bi].astype(F32)
        h_attn = _rmsnorm_cast(xb, attn_norm[0].astype(F32))
        for l in range(depth):
            lambda_init = 0.8 - 0.6 * math.exp(-0.3 * l)
            proj = _projection(_in_proj_kernel, h_attn, w_in, l, IN_PROJ_TM, MM_TN, rope,
                               name="in_proj")
            o_diff = _diff_attention(proj, jnp.full((1,), lambda_init, F32),
                                     lambda_q1[l].astype(F32), lambda_k1[l].astype(F32),
                                     lambda_q2[l].astype(F32), lambda_k2[l].astype(F32),
                                     diff_subln[l].astype(F32))
            o_na = _na_attention(proj, na_bias, na_plan, l)
            xb, h_mlp = _out_proj(o_diff, o_na, na_norm[l].astype(F32), w_out, l, xb,
                                  mlp_norm[l].astype(F32))
            u = _projection(_mlp_in_kernel, h_mlp, w_mlp_in, l, MLP_IN_TM, MLP_IN_TN,
                            name="mlp_in")
            if l == depth - 1:
                xb = _mlp_out(u, w_mlp_out, l, xb, final_norm.astype(F32), True)
            else:
                xb, h_attn = _mlp_out(u, w_mlp_out, l, xb, attn_norm[l + 1].astype(F32), False)
        outs.append(xb)
    out = outs[0][None] if b == 1 else jnp.stack(outs)
    return out.astype(x.dtype)
```

```python
import functools
import math

import jax
import jax.numpy as jnp
import numpy as np
from jax import lax
from jax.experimental import pallas as pl
from jax.experimental.pallas import tpu as pltpu

F32 = jnp.float32
BF16 = jnp.bfloat16

GRID_W = 64
N_DIFF_HEADS = 8
DIFF_QK_DIM = 64
DIFF_V_DIM = 128
DIFF_W = N_DIFF_HEADS * DIFF_V_DIM
N_NA_HEADS = 8
NA_HEAD_DIM = 128
NA_W = N_NA_HEADS * NA_HEAD_DIM
NA_WIN_ROWS = 8
NA_WIN_COLS = 16
ROPE_THETA = 10000.0
EPS = 1e-6
MASK_VALUE = -1e30
LOG2_E = math.log2(math.e)

LANES = 128
VMEM_LIMIT_BYTES = 56 * 1024 * 1024

MM_TM = 1024
MM_TN = 1024
IN_PROJ_TM = 2048
MLP_IN_TM = 2048
MLP_IN_TN = 1024
OUT_PROJ_TM = 512
OUT_PROJ_ROW_SPLIT = 2
MM_ROW_SPLIT = 4
MLP_OUT_TK = 1024
ATT_TQ = 256
ATT_TK = 512
ATT_LOOKAHEAD = 2
ATT_SCORE_BUFFERS = 4


def _compiler_params(semantics, flags=None):
    return pltpu.CompilerParams(dimension_semantics=semantics,
                                vmem_limit_bytes=VMEM_LIMIT_BYTES, flags=flags)


def _rmsnorm_rows(x, g):
    return x * lax.rsqrt(jnp.mean(x * x, axis=-1, keepdims=True) + EPS) * g


def _rmsnorm_cast_kernel(x_ref, g_ref, o_ref):
    o_ref[...] = _rmsnorm_rows(x_ref[...], g_ref[...]).astype(o_ref.dtype)


def _rmsnorm_cast(x, g):
    s, d = x.shape
    tm = min(MM_TM, s)
    return pl.pallas_call(
        _rmsnorm_cast_kernel,
        grid=(s // tm,),
        in_specs=[pl.BlockSpec((tm, d), lambda i: (i, 0)), pl.BlockSpec((1, d), lambda i: (0, 0))],
        out_specs=pl.BlockSpec((tm, d), lambda i: (i, 0)),
        out_shape=jax.ShapeDtypeStruct((s, d), BF16),
        compiler_params=_compiler_params(("arbitrary",)),
        name="first_norm",
    )(x, g.reshape(1, d))


def _cast_weights_once(w_ref, w_bf16_ref):
    @pl.when(pl.program_id(1) == 0)
    def _():
        w_bf16_ref[...] = w_ref[...].astype(BF16)


def _matmul_row_blocks(h_ref, w_ref, epilogue):
    tr = h_ref.shape[0] // MM_ROW_SPLIT
    for r in range(MM_ROW_SPLIT):
        rows = slice(r * tr, (r + 1) * tr)
        epilogue(rows, jnp.dot(h_ref[rows, :], w_ref[...], preferred_element_type=F32))


def _in_proj_kernel(h_ref, w_f32_ref, cos_ref, sin_lo_ref, sin_hi_ref, o_ref, w_ref):
    j = pl.program_id(0)
    n_rope_blocks = 2 * DIFF_W // MM_TN
    n_q_blocks = DIFF_W // MM_TN
    _cast_weights_once(w_f32_ref, w_ref)

    @pl.when(j < n_rope_blocks)
    def _():
        scale = jnp.where(j < n_q_blocks, LOG2_E * DIFF_QK_DIM ** -0.5, 1.0).astype(F32)

        def rope(rows, acc):
            cos = cos_ref[rows, :] * scale
            sin_lo = sin_lo_ref[rows, :] * scale
            sin_hi = sin_hi_ref[rows, :] * scale
            for c in range(MM_TN // LANES):
                xs = acc[:, c * LANES:(c + 1) * LANES]
                r = (xs * cos
                     + pltpu.roll(xs, LANES - DIFF_QK_DIM // 2, 1) * sin_lo
                     + pltpu.roll(xs, DIFF_QK_DIM // 2, 1) * sin_hi)
                o_ref[rows, c * LANES:(c + 1) * LANES] = r.astype(o_ref.dtype)

        _matmul_row_blocks(h_ref, w_ref, rope)

    @pl.when(j >= n_rope_blocks)
    def _():
        is_na_q = (j >= 3 * DIFF_W // MM_TN) & (j < (3 * DIFF_W + NA_W) // MM_TN)
        scale = jnp.where(is_na_q, LOG2_E * NA_HEAD_DIM ** -0.5, 1.0).astype(F32)

        def scaled(rows, acc):
            o_ref[rows, :] = (acc * scale).astype(o_ref.dtype)

        _matmul_row_blocks(h_ref, w_ref, scaled)


def _mlp_in_kernel(h_ref, w_f32_ref, o_ref, w_ref):
    _cast_weights_once(w_f32_ref, w_ref)

    def relu2(rows, acc):
        u = jnp.maximum(acc, 0.0)
        o_ref[rows, :] = (u * u).astype(o_ref.dtype)

    _matmul_row_blocks(h_ref, w_ref, relu2)


def _projection(kernel_fn, h, w, layer, tm, tn, extra=(), name=None):
    s, d = h.shape
    n = w.shape[2]
    tm = min(tm, s)
    extra_specs = [pl.BlockSpec((tm, LANES), lambda j, i: (i, 0)) for _ in extra]
    return pl.pallas_call(
        kernel_fn,
        grid=(n // tn, s // tm),
        in_specs=[pl.BlockSpec((tm, d), lambda j, i: (i, 0)),
                  pl.BlockSpec((None, d, tn), lambda j, i: (layer, 0, j))] + extra_specs,
        out_specs=pl.BlockSpec((tm, tn), lambda j, i: (i, j)),
        out_shape=jax.ShapeDtypeStruct((s, n), BF16),
        scratch_shapes=[pltpu.VMEM((d, tn), BF16)],
        compiler_params=_compiler_params(("arbitrary", "arbitrary")),
        name=name,
    )(h, w, *extra)


def _diff_attn_kernel(lam_ref, q_ref, k_ref, v_ref, lq1_ref, lk1_ref, lq2_ref, lk2_ref,
                      g_ref, o_ref, vt_ref, qpad_ref, s_ref, acc_ref,
                      *, tq, tk, n_chunks, n_blocks):
    for c in range(n_chunks):
        vt_ref[c] = v_ref[c * tk:(c + 1) * tk, :].astype(F32).T.astype(BF16)

    lookahead = ATT_LOOKAHEAD
    lambda_init = lam_ref[0]
    lam = (jnp.exp(jnp.sum(lq1_ref[...] * lk1_ref[...], axis=-1, keepdims=True))
           - jnp.exp(jnp.sum(lq2_ref[...] * lk2_ref[...], axis=-1, keepdims=True))
           + lambda_init)

    def load_queries(i):
        qt = q_ref[pl.ds(pl.multiple_of(i * tq, tq), tq), :].astype(F32).T
        row = lax.broadcasted_iota(jnp.int32, qt.shape, 0)
        qpad_ref[...] = jnp.concatenate([jnp.where(row < DIFF_QK_DIM, qt, 0.0),
                                         jnp.where(row >= DIFF_QK_DIM, qt, 0.0)],
                                        axis=1).astype(BF16)

    def scores(c):
        s = jnp.dot(k_ref[c * tk:(c + 1) * tk, :], qpad_ref[...],
                    preferred_element_type=F32)
        s_ref[c % ATT_SCORE_BUFFERS] = s
        return jnp.max(s, axis=0, keepdims=True)

    def softmax_chunk(c, cmax, m, l):
        m_new = jnp.maximum(m, cmax)
        alpha = jnp.exp2(m - m_new)
        p = jnp.exp2(s_ref[c % ATT_SCORE_BUFFERS] - m_new)
        l = alpha * l + jnp.sum(p, axis=0, keepdims=True)
        return m_new, l, alpha, p.astype(BF16)

    def accumulate(c, alpha, p):
        pv = jnp.dot(vt_ref[c], p, preferred_element_type=F32)
        acc_ref[...] = acc_ref[...] * alpha + pv

    def start_block(i):
        load_queries(i)
        return tuple(scores(c) for c in range(lookahead))

    def block(i, first_cmax):
        acc_ref[...] = jnp.zeros_like(acc_ref)
        m = jnp.full((1, 2 * tq), MASK_VALUE, F32)
        l = jnp.zeros((1, 2 * tq), F32)
        cmax = dict(enumerate(first_cmax))
        for c in range(n_chunks):
            if c + lookahead < n_chunks:
                cmax[c + lookahead] = scores(c + lookahead)
            m, l, alpha, p = softmax_chunk(c, cmax.pop(c), m, l)
            accumulate(c, alpha, p)
        next_cmax = start_block(jnp.minimum(i + 1, n_blocks - 1))

        o = acc_ref[...] / l
        od = (o[:, :tq] - lam * o[:, tq:]).T
        y = _rmsnorm_rows(od, g_ref[...]) * (1.0 - lambda_init)
        o_ref[pl.ds(pl.multiple_of(i * tq, tq), tq), :] = y.astype(o_ref.dtype)
        return next_cmax

    lax.fori_loop(0, n_blocks, block, start_block(0))


def _diff_attention(proj, lam_init, lq1, lk1, lq2, lk2, subln_g):
    s = proj.shape[0]
    tq = min(ATT_TQ, s)
    tk = min(ATT_TK, s)
    n_chunks = s // tk
    assert n_chunks % ATT_SCORE_BUFFERS == 0, "score slots must line up across query blocks"
    assert ATT_SCORE_BUFFERS >= 2 * ATT_LOOKAHEAD
    kernel_fn = functools.partial(_diff_attn_kernel, tq=tq, tk=tk, n_chunks=n_chunks,
                                  n_blocks=s // tq)
    k_col0 = DIFF_W // LANES
    v_col0 = 2 * DIFF_W // LANES
    vec = lambda n: pl.BlockSpec((1, n), lambda h: (0, 0))
    return pl.pallas_call(
        kernel_fn,
        grid=(N_DIFF_HEADS,),
        in_specs=[pl.BlockSpec(memory_space=pltpu.SMEM),
                  pl.BlockSpec((s, LANES), lambda h: (0, h)),
                  pl.BlockSpec((s, LANES), lambda h: (0, k_col0 + h)),
                  pl.BlockSpec((s, LANES), lambda h: (0, v_col0 + h)),
                  vec(DIFF_QK_DIM), vec(DIFF_QK_DIM), vec(DIFF_QK_DIM), vec(DIFF_QK_DIM),
                  vec(DIFF_V_DIM)],
        out_specs=pl.BlockSpec((s, LANES), lambda h: (0, h)),
        out_shape=jax.ShapeDtypeStruct((s, DIFF_W), BF16),
        scratch_shapes=[pltpu.VMEM((n_chunks, DIFF_V_DIM, tk), BF16),
                        pltpu.VMEM((LANES, 2 * tq), BF16),
                        pltpu.VMEM((ATT_SCORE_BUFFERS, tk, 2 * tq), F32),
                        pltpu.VMEM((DIFF_V_DIM, 2 * tq), F32)],
        compiler_params=_compiler_params(("arbitrary",)),
        name="diff_attention",
    )(lam_init, proj, proj, proj,
      lq1.reshape(1, -1), lk1.reshape(1, -1), lq2.reshape(1, -1), lk2.reshape(1, -1),
      subln_g.reshape(1, -1))


NA_GROUP = 4
NA_SPAN = 3 * NA_GROUP
NA_LOOKAHEAD = 2


def _na_span_start(g, n_groups):
    return jnp.clip(g - 1, 0, n_groups - 3)


def _na_attn_kernel(plan_ref, q_ref, k0_ref, k1_ref, k2_ref, v0_ref, v1_ref, v2_ref, bias_ref,
                    o_ref, kwin_ref, vwin_ref, *, n_groups):
    blk = NA_GROUP * GRID_W
    for t, (k_ref, v_ref) in enumerate(((k0_ref, v0_ref), (k1_ref, v1_ref), (k2_ref, v2_ref))):
        kwin_ref[t * blk:(t + 1) * blk, :] = k_ref[...]
        vwin_ref[t * blk:(t + 1) * blk, :] = v_ref[...]

    g = pl.program_id(0)
    variant = jnp.where(g == 0, 0, jnp.where(g == n_groups - 1, 2, 1))
    pairs = NA_SPAN // 2
    lane_half = lax.broadcasted_iota(jnp.int32, (1, 2 * GRID_W), 1) // GRID_W
    starts, row_masks = [], []
    for j in range(NA_GROUP):
        base = (variant * NA_GROUP + j) * 3
        starts.append(plan_ref[base])
        lo, hi = plan_ref[base + 1], plan_ref[base + 2]
        u = [2 * t + lane_half for t in range(pairs)]
        row_masks.append([(ut >= lo) & (ut < hi) for ut in u])

    def bias_rows(h, j):
        tiles = [jnp.where(row_masks[j][t], bias_ref[h, starts[j] + 2 * t], MASK_VALUE)
                 for t in range(pairs)]
        return jnp.concatenate(tiles, axis=1)

    def head_cols(h):
        return slice(h * NA_HEAD_DIM, (h + 1) * NA_HEAD_DIM)

    def scores(h):
        sc = lax.dot_general(q_ref[:, head_cols(h)], kwin_ref[:, head_cols(h)],
                             (((1,), (1,)), ((), ())),
                             preferred_element_type=F32)
        return sc + jnp.concatenate([bias_rows(h, j) for j in range(NA_GROUP)], axis=0)

    pending = {h: scores(h) for h in range(NA_LOOKAHEAD)}
    for h in range(N_NA_HEADS):
        if h + NA_LOOKAHEAD < N_NA_HEADS:
            pending[h + NA_LOOKAHEAD] = scores(h + NA_LOOKAHEAD)
        sc = pending.pop(h)
        p = jnp.exp2(sc - jnp.max(sc, axis=-1, keepdims=True))
        l = jnp.sum(p, axis=-1, keepdims=True)
        pv = jnp.dot(p.astype(BF16), vwin_ref[:, head_cols(h)], preferred_element_type=F32)
        o_ref[:, head_cols(h)] = pv / l


def _na_window_plan(rows):
    n_groups = rows // NA_GROUP
    kr = NA_WIN_ROWS

    def plan(g):
        start = NA_GROUP * int(np.clip(g - 1, 0, n_groups - 3))
        out = np.full((NA_GROUP, NA_SPAN), -1, np.int64)
        for j in range(NA_GROUP):
            r = NA_GROUP * g + j
            rs = int(np.clip(r - kr // 2, 0, rows - kr))
            for u in range(NA_SPAN):
                if rs <= start + u < rs + kr:
                    out[j, u] = start + u - r + NA_WIN_ROWS - 1
        assert (out >= 0).sum() == NA_GROUP * kr
        return out

    plans = [plan(0), plan(1), plan(n_groups - 1)]
    for g in range(1, n_groups - 1):
        assert (plan(g) == plans[1]).all()
    return plans


def _na_row_windows(rows):
    out = []
    for plan in _na_window_plan(rows):
        for j in range(NA_GROUP):
            valid_u = np.nonzero(plan[j] >= 0)[0]
            assert (np.diff(valid_u) == 1).all() and (np.diff(plan[j, valid_u]) == 1).all()
            out.append((int(plan[j, valid_u[0]] - valid_u[0]), int(valid_u[0]),
                        int(valid_u[-1]) + 1))
    return out


def _na_bias_table(rpb, rows):
    c = np.arange(GRID_W)
    cs = np.clip(c - NA_WIN_COLS // 2, 0, GRID_W - NA_WIN_COLS)
    kc = np.arange(GRID_W)
    col_valid = (kc[None, :] >= cs[:, None]) & (kc[None, :] < cs[:, None] + NA_WIN_COLS)
    pad = GRID_W - NA_WIN_COLS
    rp = jnp.pad(rpb.astype(F32) * LOG2_E, ((0, 0), (0, 0), (pad, pad)))
    by_col = jnp.stack([lax.slice_in_dim(rp, NA_WIN_COLS - 1 - ci + pad,
                                         NA_WIN_COLS - 1 - ci + pad + GRID_W, axis=2)
                        for ci in range(GRID_W)], axis=2)
    by_col = jnp.where(col_valid[None, None], by_col, MASK_VALUE)
    windows = _na_row_windows(rows)
    front = max(0, -min(w[0] for w in windows))
    back = max(0, max(w[0] for w in windows) + NA_SPAN - by_col.shape[1])
    padded = jnp.pad(by_col, ((0, 0), (front, back + 1), (0, 0), (0, 0)))
    table = jnp.concatenate([padded[:, :-1], padded[:, 1:]], axis=-1)
    plan = np.array([[w[0] + front, w[1], w[2]] for w in windows], np.int32).reshape(-1)
    return table, jnp.asarray(plan)


def _na_attention(proj, bias, plan, layer):
    s = proj.shape[0]
    rows = s // GRID_W
    assert rows % NA_GROUP == 0 and rows >= NA_SPAN and NA_SPAN >= NA_GROUP + NA_WIN_ROWS - 1
    assert NA_SPAN % 2 == 0
    n_groups = rows // NA_GROUP
    blk = NA_GROUP * GRID_W
    q_col = 3 * DIFF_W // NA_W
    k_col, v_col = q_col + 1, q_col + 2
    bias = bias.reshape((-1, N_NA_HEADS) + bias.shape[1:])

    def kv_spec(t, col):
        return pl.BlockSpec((blk, NA_W),
                            lambda g, plan_ref: (_na_span_start(g, n_groups) + t, col))

    grid_spec = pltpu.PrefetchScalarGridSpec(
        num_scalar_prefetch=1,
        grid=(n_groups,),
        in_specs=[pl.BlockSpec((blk, NA_W), lambda g, plan_ref: (g, q_col))]
                 + [kv_spec(t, k_col) for t in range(3)]
                 + [kv_spec(t, v_col) for t in range(3)]
                 + [pl.BlockSpec((None,) + bias.shape[1:],
                                 lambda g, plan_ref: (layer, 0, 0, 0, 0))],
        out_specs=pl.BlockSpec((blk, NA_W), lambda g, plan_ref: (g, 0)),
        scratch_shapes=[pltpu.VMEM((NA_SPAN * GRID_W, NA_W), BF16),
                        pltpu.VMEM((NA_SPAN * GRID_W, NA_W), BF16)])
    return pl.pallas_call(
        functools.partial(_na_attn_kernel, n_groups=n_groups),
        grid_spec=grid_spec,
        out_shape=jax.ShapeDtypeStruct((s, NA_W), F32),
        compiler_params=_compiler_params(("arbitrary",)),
        name="na_attention",
    )(plan, *([proj] * 7), bias)


def _out_proj_kernel(a_ref, b_ref, na_g_ref, w_ref, x_ref, mlp_g_ref, o_ref, h_ref):
    ka = a_ref.shape[1]
    tr = a_ref.shape[0] // OUT_PROJ_ROW_SPLIT
    row_slices = [slice(r * tr, (r + 1) * tr) for r in range(OUT_PROJ_ROW_SPLIT)]

    def normed_b(r):
        return _rmsnorm_rows(b_ref[row_slices[r], :], na_g_ref[...]).astype(BF16)

    b = normed_b(0)
    for r, rows in enumerate(row_slices):
        b_next = normed_b(r + 1) if r + 1 < OUT_PROJ_ROW_SPLIT else None
        acc = jnp.dot(a_ref[rows, :], w_ref[:ka, :], preferred_element_type=F32)
        acc += jnp.dot(b, w_ref[ka:, :], preferred_element_type=F32)
        x_new = x_ref[rows, :] + acc
        o_ref[rows, :] = x_new
        h_ref[rows, :] = _rmsnorm_rows(x_new, mlp_g_ref[...]).astype(h_ref.dtype)
        b = b_next


def _out_proj(o_diff, o_na, na_g, w, layer, x, mlp_g):
    s, d = x.shape
    tm = min(OUT_PROJ_TM, s)
    ka, kb = o_diff.shape[1], o_na.shape[1]
    row_block = lambda width: pl.BlockSpec((tm, width), lambda i: (i, 0))
    vec = lambda width: pl.BlockSpec((1, width), lambda i: (0, 0))
    return pl.pallas_call(
        _out_proj_kernel,
        grid=(s // tm,),
        in_specs=[row_block(ka), row_block(kb), vec(kb),
                  pl.BlockSpec((None, ka + kb, d), lambda i: (layer, 0, 0)),
                  row_block(d), vec(d)],
        out_specs=(row_block(d), row_block(d)),
        out_shape=(jax.ShapeDtypeStruct((s, d), F32), jax.ShapeDtypeStruct((s, d), BF16)),
        compiler_params=_compiler_params(("arbitrary",)),
        name="out_proj",
    )(o_diff, o_na, na_g.reshape(1, kb), w, x, mlp_g.reshape(1, d))


def _mlp_out_kernel(u_ref, w_ref, x_ref, g_ref, o_ref, *maybe_h_ref):
    k = pl.program_id(1)
    is_last = k == pl.num_programs(1) - 1

    @pl.when(k == 0)
    def _():
        o_ref[...] = x_ref[...] + jnp.dot(u_ref[...], w_ref[...], preferred_element_type=F32)

    @pl.when((k > 0) & jnp.logical_not(is_last))
    def _():
        o_ref[...] += jnp.dot(u_ref[...], w_ref[...], preferred_element_type=F32)

    @pl.when(is_last)
    def _():
        tr = o_ref.shape[0] // MM_ROW_SPLIT
        for r in range(MM_ROW_SPLIT):
            rows = slice(r * tr, (r + 1) * tr)
            x_new = o_ref[rows, :] + jnp.dot(u_ref[rows, :], w_ref[...],
                                             preferred_element_type=F32)
            normed = _rmsnorm_rows(x_new, g_ref[...])
            if maybe_h_ref:
                o_ref[rows, :] = x_new
                maybe_h_ref[0][rows, :] = normed.astype(maybe_h_ref[0].dtype)
            else:
                o_ref[rows, :] = normed


def _mlp_out(u, w, layer, x, g, is_last_layer):
    s, d = x.shape
    f = u.shape[1]
    assert f // MLP_OUT_TK >= 2, "the first and the last contraction step must differ"
    tm = min(MM_TM, s)
    row_block = pl.BlockSpec((tm, d), lambda i, k: (i, 0))
    x_shape = jax.ShapeDtypeStruct((s, d), F32)
    return pl.pallas_call(
        _mlp_out_kernel,
        grid=(s // tm, f // MLP_OUT_TK),
        in_specs=[pl.BlockSpec((tm, MLP_OUT_TK), lambda i, k: (i, k)),
                  pl.BlockSpec((None, MLP_OUT_TK, d), lambda i, k: (layer, k, 0)),
                  row_block, pl.BlockSpec((1, d), lambda i, k: (0, 0))],
        out_specs=row_block if is_last_layer else (row_block, row_block),
        out_shape=x_shape if is_last_layer else (x_shape, jax.ShapeDtypeStruct((s, d), BF16)),
        compiler_params=_compiler_params(("arbitrary", "arbitrary")),
        name="mlp_out_final" if is_last_layer else "mlp_out",
    )(u, w, x, g.reshape(1, d))


def _rope_tables(seq):
    inv_freq = 1.0 / (ROPE_THETA ** (jnp.arange(0, DIFF_QK_DIM, 2, dtype=F32) / DIFF_QK_DIM))
    ang = jnp.arange(seq, dtype=F32)[:, None] * inv_freq[None, :]
    cos, sin = jnp.cos(ang), jnp.sin(ang)
    zero = jnp.zeros_like(sin)
    reps = LANES // DIFF_QK_DIM
    cos_t = jnp.tile(cos, (1, 2 * reps))
    sin_lo = jnp.tile(jnp.concatenate([-sin, zero], axis=1), (1, reps))
    sin_hi = jnp.tile(jnp.concatenate([zero, sin], axis=1), (1, reps))
    return cos_t, sin_lo, sin_hi


def kernel(x, attn_norm, w_in, lambda_q1, lambda_k1, lambda_q2, lambda_k2, diff_subln, na_norm,
           na_rpb, w_out, mlp_norm, w_mlp_in, w_mlp_out, final_norm):
    b, s, d = x.shape
    depth = w_in.shape[0]
    rope = _rope_tables(s)
    na_bias, na_plan = _na_bias_table(
        na_rpb.reshape((depth * N_NA_HEADS,) + na_rpb.shape[2:]), s // GRID_W)
    w_out, w_mlp_out = w_out.astype(BF16), w_mlp_out.astype(BF16)
    w_in, w_mlp_in = w_in.astype(F32), w_mlp_in.astype(F32)
    outs = []
    for bi in range(b):
        xb = x[bi].astype(F32)
        h_attn = _rmsnorm_cast(xb, attn_norm[0].astype(F32))
        for l in range(depth):
            lambda_init = 0.8 - 0.6 * math.exp(-0.3 * l)
            proj = _projection(_in_proj_kernel, h_attn, w_in, l, IN_PROJ_TM, MM_TN, rope,
                               name="in_proj")
            o_diff = _diff_attention(proj, jnp.full((1,), lambda_init, F32),
                                     lambda_q1[l].astype(F32), lambda_k1[l].astype(F32),
                                     lambda_q2[l].astype(F32), lambda_k2[l].astype(F32),
                                     diff_subln[l].astype(F32))
            o_na = _na_attention(proj, na_bias, na_plan, l)
            xb, h_mlp = _out_proj(o_diff, o_na, na_norm[l].astype(F32), w_out, l, xb,
                                  mlp_norm[l].astype(F32))
            u = _projection(_mlp_in_kernel, h_mlp, w_mlp_in, l, MLP_IN_TM, MLP_IN_TN,
                            name="mlp_in")
            if l == depth - 1:
                xb = _mlp_out(u, w_mlp_out, l, xb, final_norm.astype(F32), True)
            else:
                xb, h_attn = _mlp_out(u, w_mlp_out, l, xb, attn_norm[l + 1].astype(F32), False)
        outs.append(xb)
    out = outs[0][None] if b == 1 else jnp.stack(outs)
    return out.astype(x.dtype)
```

```python
import functools
import math

import jax
import jax.numpy as jnp
import numpy as np
from jax import lax
from jax.experimental import pallas as pl
from jax.experimental.pallas import tpu as pltpu

F32 = jnp.float32
BF16 = jnp.bfloat16

GRID_W = 64
N_DIFF_HEADS = 8
DIFF_QK_DIM = 64
DIFF_V_DIM = 128
DIFF_W = N_DIFF_HEADS * DIFF_V_DIM
N_NA_HEADS = 8
NA_HEAD_DIM = 128
NA_W = N_NA_HEADS * NA_HEAD_DIM
NA_WIN_ROWS = 8
NA_WIN_COLS = 16
ROPE_THETA = 10000.0
EPS = 1e-6
MASK_VALUE = -1e30
LOG2_E = math.log2(math.e)

LANES = 128
VMEM_LIMIT_BYTES = 56 * 1024 * 1024

MM_TM = 1024
MM_TN = 1024
IN_PROJ_TM = 2048
MLP_IN_TM = 2048
MLP_IN_TN = 1024
OUT_PROJ_TM = 512
OUT_PROJ_ROW_SPLIT = 2
MM_ROW_SPLIT = 4
MLP_OUT_TK = 1024
ATT_TQ = 256
ATT_TK = 512
ATT_LOOKAHEAD = 2
ATT_SCORE_BUFFERS = 4


def _compiler_params(semantics, flags=None):
    return pltpu.CompilerParams(dimension_semantics=semantics,
                                vmem_limit_bytes=VMEM_LIMIT_BYTES, flags=flags)


def _rmsnorm_rows(x, g):
    return x * lax.rsqrt(jnp.mean(x * x, axis=-1, keepdims=True) + EPS) * g


def _rmsnorm_cast_kernel(x_ref, g_ref, o_ref):
    o_ref[...] = _rmsnorm_rows(x_ref[...], g_ref[...]).astype(o_ref.dtype)


def _rmsnorm_cast(x, g):
    s, d = x.shape
    tm = min(MM_TM, s)
    return pl.pallas_call(
        _rmsnorm_cast_kernel,
        grid=(s // tm,),
        in_specs=[pl.BlockSpec((tm, d), lambda i: (i, 0)), pl.BlockSpec((1, d), lambda i: (0, 0))],
        out_specs=pl.BlockSpec((tm, d), lambda i: (i, 0)),
        out_shape=jax.ShapeDtypeStruct((s, d), BF16),
        compiler_params=_compiler_params(("arbitrary",)),
        name="first_norm",
    )(x, g.reshape(1, d))


def _cast_weights_once(w_ref, w_bf16_ref):
    @pl.when(pl.program_id(1) == 0)
    def _():
        w_bf16_ref[...] = w_ref[...].astype(BF16)


def _matmul_row_blocks(h_ref, w_ref, epilogue):
    tr = h_ref.shape[0] // MM_ROW_SPLIT
    for r in range(MM_ROW_SPLIT):
        rows = slice(r * tr, (r + 1) * tr)
        epilogue(rows, jnp.dot(h_ref[rows, :], w_ref[...], preferred_element_type=F32))


def _in_proj_kernel(h_ref, w_f32_ref, cos_ref, sin_lo_ref, sin_hi_ref, o_ref, w_ref):
    j = pl.program_id(0)
    n_rope_blocks = 2 * DIFF_W // MM_TN
    n_q_blocks = DIFF_W // MM_TN
    _cast_weights_once(w_f32_ref, w_ref)

    @pl.when(j < n_rope_blocks)
    def _():
        scale = jnp.where(j < n_q_blocks, LOG2_E * DIFF_QK_DIM ** -0.5, 1.0).astype(F32)

        def rope(rows, acc):
            cos = cos_ref[rows, :] * scale
            sin_lo = sin_lo_ref[rows, :] * scale
            sin_hi = sin_hi_ref[rows, :] * scale
            for c in range(MM_TN // LANES):
                xs = acc[:, c * LANES:(c + 1) * LANES]
                r = (xs * cos
                     + pltpu.roll(xs, LANES - DIFF_QK_DIM // 2, 1) * sin_lo
                     + pltpu.roll(xs, DIFF_QK_DIM // 2, 1) * sin_hi)
                o_ref[rows, c * LANES:(c + 1) * LANES] = r.astype(o_ref.dtype)

        _matmul_row_blocks(h_ref, w_ref, rope)

    @pl.when(j >= n_rope_blocks)
    def _():
        is_na_q = (j >= 3 * DIFF_W // MM_TN) & (j < (3 * DIFF_W + NA_W) // MM_TN)
        scale = jnp.where(is_na_q, LOG2_E * NA_HEAD_DIM ** -0.5, 1.0).astype(F32)

        def scaled(rows, acc):
            o_ref[rows, :] = (acc * scale).astype(o_ref.dtype)

        _matmul_row_blocks(h_ref, w_ref, scaled)


def _mlp_in_kernel(h_ref, w_f32_ref, w_out_f32_ref, o_ref, w_out_bf16_ref, w_ref):
    _cast_weights_once(w_f32_ref, w_ref)
    w_out_bf16_ref[...] = w_out_f32_ref[...].astype(BF16)

    def relu2(rows, acc):
        u = jnp.maximum(acc, 0.0)
        o_ref[rows, :] = (u * u).astype(o_ref.dtype)

    _matmul_row_blocks(h_ref, w_ref, relu2)


def _mlp_in(h, w, w_out, layer):
    s, d = h.shape
    f = w.shape[2]
    tm = min(MLP_IN_TM, s)
    n_i = s // tm
    n_steps = (f // MLP_IN_TN) * n_i
    slab = f // n_steps
    assert slab * n_steps == f and slab % 16 == 0
    return pl.pallas_call(
        _mlp_in_kernel,
        grid=(f // MLP_IN_TN, n_i),
        in_specs=[pl.BlockSpec((tm, d), lambda j, i: (i, 0)),
                  pl.BlockSpec((None, d, MLP_IN_TN), lambda j, i: (layer, 0, j)),
                  pl.BlockSpec((None, slab, d), lambda j, i: (layer, j * n_i + i, 0))],
        out_specs=(pl.BlockSpec((tm, MLP_IN_TN), lambda j, i: (i, j)),
                   pl.BlockSpec((slab, d), lambda j, i: (j * n_i + i, 0))),
        out_shape=(jax.ShapeDtypeStruct((s, f), BF16), jax.ShapeDtypeStruct((f, d), BF16)),
        scratch_shapes=[pltpu.VMEM((d, MLP_IN_TN), BF16)],
        compiler_params=_compiler_params(("arbitrary", "arbitrary")),
        name="mlp_in",
    )(h, w, w_out)


def _projection(kernel_fn, h, w, layer, tm, tn, extra=(), name=None):
    s, d = h.shape
    n = w.shape[2]
    tm = min(tm, s)
    extra_specs = [pl.BlockSpec((tm, LANES), lambda j, i: (i, 0)) for _ in extra]
    return pl.pallas_call(
        kernel_fn,
        grid=(n // tn, s // tm),
        in_specs=[pl.BlockSpec((tm, d), lambda j, i: (i, 0)),
                  pl.BlockSpec((None, d, tn), lambda j, i: (layer, 0, j))] + extra_specs,
        out_specs=pl.BlockSpec((tm, tn), lambda j, i: (i, j)),
        out_shape=jax.ShapeDtypeStruct((s, n), BF16),
        scratch_shapes=[pltpu.VMEM((d, tn), BF16)],
        compiler_params=_compiler_params(("arbitrary", "arbitrary")),
        name=name,
    )(h, w, *extra)


def _diff_attn_kernel(lam_ref, q_ref, k_ref, v_ref, lq1_ref, lk1_ref, lq2_ref, lk2_ref,
                      g_ref, o_ref, vt_ref, qpad_ref, s_ref, acc_ref,
                      *, tq, tk, n_chunks, n_blocks):
    for c in range(n_chunks):
        vt_ref[c] = v_ref[c * tk:(c + 1) * tk, :].astype(F32).T.astype(BF16)

    lookahead = ATT_LOOKAHEAD
    lambda_init = lam_ref[0]
    lam = (jnp.exp(jnp.sum(lq1_ref[...] * lk1_ref[...], axis=-1, keepdims=True))
           - jnp.exp(jnp.sum(lq2_ref[...] * lk2_ref[...], axis=-1, keepdims=True))
           + lambda_init)

    def load_queries(i):
        qt = q_ref[pl.ds(pl.multiple_of(i * tq, tq), tq), :].astype(F32).T
        row = lax.broadcasted_iota(jnp.int32, qt.shape, 0)
        qpad_ref[...] = jnp.concatenate([jnp.where(row < DIFF_QK_DIM, qt, 0.0),
                                         jnp.where(row >= DIFF_QK_DIM, qt, 0.0)],
                                        axis=1).astype(BF16)

    def scores(c):
        s = jnp.dot(k_ref[c * tk:(c + 1) * tk, :], qpad_ref[...],
                    preferred_element_type=F32)
        s_ref[c % ATT_SCORE_BUFFERS] = s
        return jnp.max(s, axis=0, keepdims=True)

    def softmax_chunk(c, cmax, m, l):
        m_new = jnp.maximum(m, cmax)
        alpha = jnp.exp2(m - m_new)
        p = jnp.exp2(s_ref[c % ATT_SCORE_BUFFERS] - m_new)
        l = alpha * l + jnp.sum(p, axis=0, keepdims=True)
        return m_new, l, alpha, p.astype(BF16)

    def accumulate(c, alpha, p):
        pv = jnp.dot(vt_ref[c], p, preferred_element_type=F32)
        acc_ref[...] = acc_ref[...] * alpha + pv

    def start_block(i):
        load_queries(i)
        return tuple(scores(c) for c in range(lookahead))

    def block(i, first_cmax):
        acc_ref[...] = jnp.zeros_like(acc_ref)
        m = jnp.full((1, 2 * tq), MASK_VALUE, F32)
        l = jnp.zeros((1, 2 * tq), F32)
        cmax = dict(enumerate(first_cmax))
        for c in range(n_chunks):
            if c + lookahead < n_chunks:
                cmax[c + lookahead] = scores(c + lookahead)
            m, l, alpha, p = softmax_chunk(c, cmax.pop(c), m, l)
            accumulate(c, alpha, p)
        next_cmax = start_block(jnp.minimum(i + 1, n_blocks - 1))

        o = acc_ref[...] / l
        od = (o[:, :tq] - lam * o[:, tq:]).T
        y = _rmsnorm_rows(od, g_ref[...]) * (1.0 - lambda_init)
        o_ref[pl.ds(pl.multiple_of(i * tq, tq), tq), :] = y.astype(o_ref.dtype)
        return next_cmax

    lax.fori_loop(0, n_blocks, block, start_block(0))


def _diff_attention(proj, lam_init, lq1, lk1, lq2, lk2, subln_g):
    s = proj.shape[0]
    tq = min(ATT_TQ, s)
    tk = min(ATT_TK, s)
    n_chunks = s // tk
    assert n_chunks % ATT_SCORE_BUFFERS == 0, "score slots must line up across query blocks"
    assert ATT_SCORE_BUFFERS >= 2 * ATT_LOOKAHEAD
    kernel_fn = functools.partial(_diff_attn_kernel, tq=tq, tk=tk, n_chunks=n_chunks,
                                  n_blocks=s // tq)
    k_col0 = DIFF_W // LANES
    v_col0 = 2 * DIFF_W // LANES
    vec = lambda n: pl.BlockSpec((1, n), lambda h: (0, 0))
    return pl.pallas_call(
        kernel_fn,
        grid=(N_DIFF_HEADS,),
        in_specs=[pl.BlockSpec(memory_space=pltpu.SMEM),
                  pl.BlockSpec((s, LANES), lambda h: (0, h)),
                  pl.BlockSpec((s, LANES), lambda h: (0, k_col0 + h)),
                  pl.BlockSpec((s, LANES), lambda h: (0, v_col0 + h)),
                  vec(DIFF_QK_DIM), vec(DIFF_QK_DIM), vec(DIFF_QK_DIM), vec(DIFF_QK_DIM),
                  vec(DIFF_V_DIM)],
        out_specs=pl.BlockSpec((s, LANES), lambda h: (0, h)),
        out_shape=jax.ShapeDtypeStruct((s, DIFF_W), BF16),
        scratch_shapes=[pltpu.VMEM((n_chunks, DIFF_V_DIM, tk), BF16),
                        pltpu.VMEM((LANES, 2 * tq), BF16),
                        pltpu.VMEM((ATT_SCORE_BUFFERS, tk, 2 * tq), F32),
                        pltpu.VMEM((DIFF_V_DIM, 2 * tq), F32)],
        compiler_params=_compiler_params(("arbitrary",)),
        name="diff_attention",
    )(lam_init, proj, proj, proj,
      lq1.reshape(1, -1), lk1.reshape(1, -1), lq2.reshape(1, -1), lk2.reshape(1, -1),
      subln_g.reshape(1, -1))


NA_GROUP = 4
NA_SPAN = 3 * NA_GROUP
NA_LOOKAHEAD = 2


def _na_span_start(g, n_groups):
    return jnp.clip(g - 1, 0, n_groups - 3)


def _na_attn_kernel(plan_ref, q_ref, k0_ref, k1_ref, k2_ref, v0_ref, v1_ref, v2_ref, bias_ref,
                    o_ref, kwin_ref, vwin_ref, *, n_groups):
    blk = NA_GROUP * GRID_W
    for t, (k_ref, v_ref) in enumerate(((k0_ref, v0_ref), (k1_ref, v1_ref), (k2_ref, v2_ref))):
        kwin_ref[t * blk:(t + 1) * blk, :] = k_ref[...]
        vwin_ref[t * blk:(t + 1) * blk, :] = v_ref[...]

    g = pl.program_id(0)
    variant = jnp.where(g == 0, 0, jnp.where(g == n_groups - 1, 2, 1))
    pairs = NA_SPAN // 2
    lane_half = lax.broadcasted_iota(jnp.int32, (1, 2 * GRID_W), 1) // GRID_W
    starts, row_masks = [], []
    for j in range(NA_GROUP):
        base = (variant * NA_GROUP + j) * 3
        starts.append(plan_ref[base])
        lo, hi = plan_ref[base + 1], plan_ref[base + 2]
        u = [2 * t + lane_half for t in range(pairs)]
        row_masks.append([(ut >= lo) & (ut < hi) for ut in u])

    def bias_rows(h, j):
        tiles = [jnp.where(row_masks[j][t], bias_ref[h, starts[j] + 2 * t], MASK_VALUE)
                 for t in range(pairs)]
        return jnp.concatenate(tiles, axis=1)

    def head_cols(h):
        return slice(h * NA_HEAD_DIM, (h + 1) * NA_HEAD_DIM)

    def scores(h):
        sc = lax.dot_general(q_ref[:, head_cols(h)], kwin_ref[:, head_cols(h)],
                             (((1,), (1,)), ((), ())),
                             preferred_element_type=F32)
        return sc + jnp.concatenate([bias_rows(h, j) for j in range(NA_GROUP)], axis=0)

    pending = {h: scores(h) for h in range(NA_LOOKAHEAD)}
    for h in range(N_NA_HEADS):
        if h + NA_LOOKAHEAD < N_NA_HEADS:
            pending[h + NA_LOOKAHEAD] = scores(h + NA_LOOKAHEAD)
        sc = pending.pop(h)
        p = jnp.exp2(sc - jnp.max(sc, axis=-1, keepdims=True))
        l = jnp.sum(p, axis=-1, keepdims=True)
        pv = jnp.dot(p.astype(BF16), vwin_ref[:, head_cols(h)], preferred_element_type=F32)
        o_ref[:, head_cols(h)] = pv / l


def _na_window_plan(rows):
    n_groups = rows // NA_GROUP
    kr = NA_WIN_ROWS

    def plan(g):
        start = NA_GROUP * int(np.clip(g - 1, 0, n_groups - 3))
        out = np.full((NA_GROUP, NA_SPAN), -1, np.int64)
        for j in range(NA_GROUP):
            r = NA_GROUP * g + j
            rs = int(np.clip(r - kr // 2, 0, rows - kr))
            for u in range(NA_SPAN):
                if rs <= start + u < rs + kr:
                    out[j, u] = start + u - r + NA_WIN_ROWS - 1
        assert (out >= 0).sum() == NA_GROUP * kr
        return out

    plans = [plan(0), plan(1), plan(n_groups - 1)]
    for g in range(1, n_groups - 1):
        assert (plan(g) == plans[1]).all()
    return plans


def _na_row_windows(rows):
    out = []
    for plan in _na_window_plan(rows):
        for j in range(NA_GROUP):
            valid_u = np.nonzero(plan[j] >= 0)[0]
            assert (np.diff(valid_u) == 1).all() and (np.diff(plan[j, valid_u]) == 1).all()
            out.append((int(plan[j, valid_u[0]] - valid_u[0]), int(valid_u[0]),
                        int(valid_u[-1]) + 1))
    return out


def _na_bias_table(rpb, rows):
    c = np.arange(GRID_W)
    cs = np.clip(c - NA_WIN_COLS // 2, 0, GRID_W - NA_WIN_COLS)
    kc = np.arange(GRID_W)
    col_valid = (kc[None, :] >= cs[:, None]) & (kc[None, :] < cs[:, None] + NA_WIN_COLS)
    pad = GRID_W - NA_WIN_COLS
    rp = jnp.pad(rpb.astype(F32) * LOG2_E, ((0, 0), (0, 0), (pad, pad)))
    by_col = jnp.stack([lax.slice_in_dim(rp, NA_WIN_COLS - 1 - ci + pad,
                                         NA_WIN_COLS - 1 - ci + pad + GRID_W, axis=2)
                        for ci in range(GRID_W)], axis=2)
    by_col = jnp.where(col_valid[None, None], by_col, MASK_VALUE)
    windows = _na_row_windows(rows)
    front = max(0, -min(w[0] for w in windows))
    back = max(0, max(w[0] for w in windows) + NA_SPAN - by_col.shape[1])
    padded = jnp.pad(by_col, ((0, 0), (front, back + 1), (0, 0), (0, 0)))
    table = jnp.concatenate([padded[:, :-1], padded[:, 1:]], axis=-1)
    plan = np.array([[w[0] + front, w[1], w[2]] for w in windows], np.int32).reshape(-1)
    return table, jnp.asarray(plan)


def _na_attention(proj, bias, plan, layer):
    s = proj.shape[0]
    rows = s // GRID_W
    assert rows % NA_GROUP == 0 and rows >= NA_SPAN and NA_SPAN >= NA_GROUP + NA_WIN_ROWS - 1
    assert NA_SPAN % 2 == 0
    n_groups = rows // NA_GROUP
    blk = NA_GROUP * GRID_W
    q_col = 3 * DIFF_W // NA_W
    k_col, v_col = q_col + 1, q_col + 2
    bias = bias.reshape((-1, N_NA_HEADS) + bias.shape[1:])

    def kv_spec(t, col):
        return pl.BlockSpec((blk, NA_W),
                            lambda g, plan_ref: (_na_span_start(g, n_groups) + t, col))

    grid_spec = pltpu.PrefetchScalarGridSpec(
        num_scalar_prefetch=1,
        grid=(n_groups,),
        in_specs=[pl.BlockSpec((blk, NA_W), lambda g, plan_ref: (g, q_col))]
                 + [kv_spec(t, k_col) for t in range(3)]
                 + [kv_spec(t, v_col) for t in range(3)]
                 + [pl.BlockSpec((None,) + bias.shape[1:],
                                 lambda g, plan_ref: (layer, 0, 0, 0, 0))],
        out_specs=pl.BlockSpec((blk, NA_W), lambda g, plan_ref: (g, 0)),
        scratch_shapes=[pltpu.VMEM((NA_SPAN * GRID_W, NA_W), BF16),
                        pltpu.VMEM((NA_SPAN * GRID_W, NA_W), BF16)])
    return pl.pallas_call(
        functools.partial(_na_attn_kernel, n_groups=n_groups),
        grid_spec=grid_spec,
        out_shape=jax.ShapeDtypeStruct((s, NA_W), F32),
        compiler_params=_compiler_params(("arbitrary",)),
        name="na_attention",
    )(plan, *([proj] * 7), bias)


def _out_proj_kernel(a_ref, b_ref, na_g_ref, w_ref, x_ref, mlp_g_ref, o_ref, h_ref):
    ka = a_ref.shape[1]
    tr = a_ref.shape[0] // OUT_PROJ_ROW_SPLIT
    row_slices = [slice(r * tr, (r + 1) * tr) for r in range(OUT_PROJ_ROW_SPLIT)]

    def normed_b(r):
        return _rmsnorm_rows(b_ref[row_slices[r], :], na_g_ref[...]).astype(BF16)

    b = normed_b(0)
    for r, rows in enumerate(row_slices):
        b_next = normed_b(r + 1) if r + 1 < OUT_PROJ_ROW_SPLIT else None
        acc = jnp.dot(a_ref[rows, :], w_ref[:ka, :], preferred_element_type=F32)
        acc += jnp.dot(b, w_ref[ka:, :], preferred_element_type=F32)
        x_new = x_ref[rows, :] + acc
        o_ref[rows, :] = x_new
        h_ref[rows, :] = _rmsnorm_rows(x_new, mlp_g_ref[...]).astype(h_ref.dtype)
        b = b_next


def _out_proj(o_diff, o_na, na_g, w, layer, x, mlp_g):
    s, d = x.shape
    tm = min(OUT_PROJ_TM, s)
    ka, kb = o_diff.shape[1], o_na.shape[1]
    row_block = lambda width: pl.BlockSpec((tm, width), lambda i: (i, 0))
    vec = lambda width: pl.BlockSpec((1, width), lambda i: (0, 0))
    return pl.pallas_call(
        _out_proj_kernel,
        grid=(s // tm,),
        in_specs=[row_block(ka), row_block(kb), vec(kb),
                  pl.BlockSpec((None, ka + kb, d), lambda i: (layer, 0, 0)),
                  row_block(d), vec(d)],
        out_specs=(row_block(d), row_block(d)),
        out_shape=(jax.ShapeDtypeStruct((s, d), F32), jax.ShapeDtypeStruct((s, d), BF16)),
        compiler_params=_compiler_params(("arbitrary",)),
        name="out_proj",
    )(o_diff, o_na, na_g.reshape(1, kb), w, x, mlp_g.reshape(1, d))


def _mlp_out_kernel(u_ref, w_ref, x_ref, g_ref, o_ref, *maybe_h_ref):
    k = pl.program_id(1)

    @pl.when(k == 0)
    def _():
        o_ref[...] = x_ref[...]

    is_last = k == pl.num_programs(1) - 1

    @pl.when(jnp.logical_not(is_last))
    def _():
        o_ref[...] += jnp.dot(u_ref[...], w_ref[...], preferred_element_type=F32)

    @pl.when(is_last)
    def _():
        tr = o_ref.shape[0] // MM_ROW_SPLIT
        for r in range(MM_ROW_SPLIT):
            rows = slice(r * tr, (r + 1) * tr)
            x_new = o_ref[rows, :] + jnp.dot(u_ref[rows, :], w_ref[...],
                                             preferred_element_type=F32)
            normed = _rmsnorm_rows(x_new, g_ref[...])
            if maybe_h_ref:
                o_ref[rows, :] = x_new
                maybe_h_ref[0][rows, :] = normed.astype(maybe_h_ref[0].dtype)
            else:
                o_ref[rows, :] = normed


def _mlp_out(u, w, x, g, is_last_layer):
    s, d = x.shape
    f = u.shape[1]
    tm = min(MM_TM, s)
    row_block = pl.BlockSpec((tm, d), lambda i, k: (i, 0))
    x_shape = jax.ShapeDtypeStruct((s, d), F32)
    return pl.pallas_call(
        _mlp_out_kernel,
        grid=(s // tm, f // MLP_OUT_TK),
        in_specs=[pl.BlockSpec((tm, MLP_OUT_TK), lambda i, k: (i, k)),
                  pl.BlockSpec((MLP_OUT_TK, d), lambda i, k: (k, 0)),
                  row_block, pl.BlockSpec((1, d), lambda i, k: (0, 0))],
        out_specs=row_block if is_last_layer else (row_block, row_block),
        out_shape=x_shape if is_last_layer else (x_shape, jax.ShapeDtypeStruct((s, d), BF16)),
        compiler_params=_compiler_params(("arbitrary", "arbitrary")),
        name="mlp_out_final" if is_last_layer else "mlp_out",
    )(u, w, x, g.reshape(1, d))


def _rope_tables(seq):
    inv_freq = 1.0 / (ROPE_THETA ** (jnp.arange(0, DIFF_QK_DIM, 2, dtype=F32) / DIFF_QK_DIM))
    ang = jnp.arange(seq, dtype=F32)[:, None] * inv_freq[None, :]
    cos, sin = jnp.cos(ang), jnp.sin(ang)
    zero = jnp.zeros_like(sin)
    reps = LANES // DIFF_QK_DIM
    cos_t = jnp.tile(cos, (1, 2 * reps))
    sin_lo = jnp.tile(jnp.concatenate([-sin, zero], axis=1), (1, reps))
    sin_hi = jnp.tile(jnp.concatenate([zero, sin], axis=1), (1, reps))
    return cos_t, sin_lo, sin_hi


def kernel(x, attn_norm, w_in, lambda_q1, lambda_k1, lambda_q2, lambda_k2, diff_subln, na_norm,
           na_rpb, w_out, mlp_norm, w_mlp_in, w_mlp_out, final_norm):
    b, s, d = x.shape
    depth = w_in.shape[0]
    rope = _rope_tables(s)
    na_bias, na_plan = _na_bias_table(
        na_rpb.reshape((depth * N_NA_HEADS,) + na_rpb.shape[2:]), s // GRID_W)
    w_out = w_out.astype(BF16)
    w_in, w_mlp_in, w_mlp_out = (w.astype(F32) for w in (w_in, w_mlp_in, w_mlp_out))
    outs = []
    for bi in range(b):
        xb = x[bi].astype(F32)
        h_attn = _rmsnorm_cast(xb, attn_norm[0].astype(F32))
        for l in range(depth):
            lambda_init = 0.8 - 0.6 * math.exp(-0.3 * l)
            proj = _projection(_in_proj_kernel, h_attn, w_in, l, IN_PROJ_TM, MM_TN, rope,
                               name="in_proj")
            o_diff = _diff_attention(proj, jnp.full((1,), lambda_init, F32),
                                     lambda_q1[l].astype(F32), lambda_k1[l].astype(F32),
                                     lambda_q2[l].astype(F32), lambda_k2[l].astype(F32),
                                     diff_subln[l].astype(F32))
            o_na = _na_attention(proj, na_bias, na_plan, l)
            xb, h_mlp = _out_proj(o_diff, o_na, na_norm[l].astype(F32), w_out, l, xb,
                                  mlp_norm[l].astype(F32))
            u, w_mlp_out_l = _mlp_in(h_mlp, w_mlp_in, w_mlp_out, l)
            if l == depth - 1:
                xb = _mlp_out(u, w_mlp_out_l, xb, final_norm.astype(F32), True)
            else:
                xb, h_attn = _mlp_out(u, w_mlp_out_l, xb, attn_norm[l + 1].astype(F32), False)
        outs.append(xb)
    out = outs[0][None] if b == 1 else jnp.stack(outs)
    return out.astype(x.dtype)
```

```python
import functools
import math

import jax
import jax.numpy as jnp
import numpy as np
from jax import lax
from jax.experimental import pallas as pl
from jax.experimental.pallas import tpu as pltpu

F32 = jnp.float32
BF16 = jnp.bfloat16

GRID_W = 64
N_DIFF_HEADS = 8
DIFF_QK_DIM = 64
DIFF_V_DIM = 128
DIFF_W = N_DIFF_HEADS * DIFF_V_DIM
N_NA_HEADS = 8
NA_HEAD_DIM = 128
NA_W = N_NA_HEADS * NA_HEAD_DIM
NA_WIN_ROWS = 8
NA_WIN_COLS = 16
ROPE_THETA = 10000.0
EPS = 1e-6
MASK_VALUE = -1e30
LOG2_E = math.log2(math.e)

LANES = 128
VMEM_LIMIT_BYTES = 56 * 1024 * 1024

MM_TM = 1024
MM_TN = 1024
IN_PROJ_TM = 2048
MLP_IN_TM = 2048
MLP_IN_TN = 1024
OUT_PROJ_TM = 512
OUT_PROJ_ROW_SPLIT = 2
MM_ROW_SPLIT = 4
MLP_OUT_TK = 1024
ATT_TQ = 256
ATT_TK = 512
ATT_LOOKAHEAD = 2
ATT_SCORE_BUFFERS = 4


def _compiler_params(semantics, flags=None):
    return pltpu.CompilerParams(dimension_semantics=semantics,
                                vmem_limit_bytes=VMEM_LIMIT_BYTES, flags=flags)


def _rmsnorm_rows(x, g):
    return x * lax.rsqrt(jnp.mean(x * x, axis=-1, keepdims=True) + EPS) * g


def _rmsnorm_cast_kernel(x_ref, g_ref, o_ref):
    o_ref[...] = _rmsnorm_rows(x_ref[...], g_ref[...]).astype(o_ref.dtype)


def _rmsnorm_cast(x, g):
    s, d = x.shape
    tm = min(MM_TM, s)
    return pl.pallas_call(
        _rmsnorm_cast_kernel,
        grid=(s // tm,),
        in_specs=[pl.BlockSpec((tm, d), lambda i: (i, 0)), pl.BlockSpec((1, d), lambda i: (0, 0))],
        out_specs=pl.BlockSpec((tm, d), lambda i: (i, 0)),
        out_shape=jax.ShapeDtypeStruct((s, d), BF16),
        compiler_params=_compiler_params(("arbitrary",)),
        name="first_norm",
    )(x, g.reshape(1, d))


def _cast_weights_once(w_ref, w_bf16_ref):
    @pl.when(pl.program_id(1) == 0)
    def _():
        w_bf16_ref[...] = w_ref[...].astype(BF16)


def _matmul_row_blocks(h_ref, w_ref, epilogue):
    tr = h_ref.shape[0] // MM_ROW_SPLIT
    for r in range(MM_ROW_SPLIT):
        rows = slice(r * tr, (r + 1) * tr)
        epilogue(rows, jnp.dot(h_ref[rows, :], w_ref[...], preferred_element_type=F32))


def _in_proj_kernel(h_ref, w_f32_ref, cos_ref, sin_lo_ref, sin_hi_ref, o_ref, w_ref):
    j = pl.program_id(0)
    n_rope_blocks = 2 * DIFF_W // MM_TN
    n_q_blocks = DIFF_W // MM_TN
    _cast_weights_once(w_f32_ref, w_ref)

    @pl.when(j < n_rope_blocks)
    def _():
        scale = jnp.where(j < n_q_blocks, LOG2_E * DIFF_QK_DIM ** -0.5, 1.0).astype(F32)

        def rope(rows, acc):
            cos = cos_ref[rows, :] * scale
            sin_lo = sin_lo_ref[rows, :] * scale
            sin_hi = sin_hi_ref[rows, :] * scale
            for c in range(MM_TN // LANES):
                xs = acc[:, c * LANES:(c + 1) * LANES]
                r = (xs * cos
                     + pltpu.roll(xs, LANES - DIFF_QK_DIM // 2, 1) * sin_lo
                     + pltpu.roll(xs, DIFF_QK_DIM // 2, 1) * sin_hi)
                o_ref[rows, c * LANES:(c + 1) * LANES] = r.astype(o_ref.dtype)

        _matmul_row_blocks(h_ref, w_ref, rope)

    @pl.when(j >= n_rope_blocks)
    def _():
        is_na_q = (j >= 3 * DIFF_W // MM_TN) & (j < (3 * DIFF_W + NA_W) // MM_TN)
        scale = jnp.where(is_na_q, LOG2_E * NA_HEAD_DIM ** -0.5, 1.0).astype(F32)

        def scaled(rows, acc):
            o_ref[rows, :] = (acc * scale).astype(o_ref.dtype)

        _matmul_row_blocks(h_ref, w_ref, scaled)


def _mlp_in_kernel(h_ref, w_f32_ref, w_out_f32_ref, o_ref, w_out_bf16_ref, w_ref):
    _cast_weights_once(w_f32_ref, w_ref)
    w_out_bf16_ref[...] = w_out_f32_ref[...].astype(BF16)

    def relu2(rows, acc):
        u = jnp.maximum(acc, 0.0)
        o_ref[rows, :] = (u * u).astype(o_ref.dtype)

    _matmul_row_blocks(h_ref, w_ref, relu2)


def _mlp_in(h, w, w_out, layer):
    s, d = h.shape
    f = w.shape[2]
    tm = min(MLP_IN_TM, s)
    n_i = s // tm
    n_steps = (f // MLP_IN_TN) * n_i
    slab = f // n_steps
    assert slab * n_steps == f and slab % 16 == 0
    return pl.pallas_call(
        _mlp_in_kernel,
        grid=(f // MLP_IN_TN, n_i),
        in_specs=[pl.BlockSpec((tm, d), lambda j, i: (i, 0)),
                  pl.BlockSpec((None, d, MLP_IN_TN), lambda j, i: (layer, 0, j)),
                  pl.BlockSpec((None, slab, d), lambda j, i: (layer, j * n_i + i, 0))],
        out_specs=(pl.BlockSpec((tm, MLP_IN_TN), lambda j, i: (i, j)),
                   pl.BlockSpec((slab, d), lambda j, i: (j * n_i + i, 0))),
        out_shape=(jax.ShapeDtypeStruct((s, f), BF16), jax.ShapeDtypeStruct((f, d), BF16)),
        scratch_shapes=[pltpu.VMEM((d, MLP_IN_TN), BF16)],
        compiler_params=_compiler_params(("arbitrary", "arbitrary")),
        name="mlp_in",
    )(h, w, w_out)


def _projection(kernel_fn, h, w, layer, tm, tn, extra=(), name=None):
    s, d = h.shape
    n = w.shape[2]
    tm = min(tm, s)
    extra_specs = [pl.BlockSpec((tm, LANES), lambda j, i: (i, 0)) for _ in extra]
    return pl.pallas_call(
        kernel_fn,
        grid=(n // tn, s // tm),
        in_specs=[pl.BlockSpec((tm, d), lambda j, i: (i, 0)),
                  pl.BlockSpec((None, d, tn), lambda j, i: (layer, 0, j))] + extra_specs,
        out_specs=pl.BlockSpec((tm, tn), lambda j, i: (i, j)),
        out_shape=jax.ShapeDtypeStruct((s, n), BF16),
        scratch_shapes=[pltpu.VMEM((d, tn), BF16)],
        compiler_params=_compiler_params(("arbitrary", "arbitrary")),
        name=name,
    )(h, w, *extra)


def _diff_attn_kernel(lam_ref, q_ref, k_ref, v_ref, lq1_ref, lk1_ref, lq2_ref, lk2_ref,
                      g_ref, o_ref, vt_ref, qpad_ref, s_ref, acc_ref,
                      *, tq, tk, n_chunks, n_blocks):
    for c in range(n_chunks):
        vt_ref[c] = v_ref[c * tk:(c + 1) * tk, :].astype(F32).T.astype(BF16)

    lookahead = ATT_LOOKAHEAD
    lambda_init = lam_ref[0]
    lam = (jnp.exp(jnp.sum(lq1_ref[...] * lk1_ref[...], axis=-1, keepdims=True))
           - jnp.exp(jnp.sum(lq2_ref[...] * lk2_ref[...], axis=-1, keepdims=True))
           + lambda_init)

    def load_queries(i):
        qt = q_ref[pl.ds(pl.multiple_of(i * tq, tq), tq), :].astype(F32).T
        row = lax.broadcasted_iota(jnp.int32, qt.shape, 0)
        qpad_ref[...] = jnp.concatenate([jnp.where(row < DIFF_QK_DIM, qt, 0.0),
                                         jnp.where(row >= DIFF_QK_DIM, qt, 0.0)],
                                        axis=1).astype(BF16)

    def scores(c):
        s = jnp.dot(k_ref[c * tk:(c + 1) * tk, :], qpad_ref[...],
                    preferred_element_type=F32)
        s_ref[c % ATT_SCORE_BUFFERS] = s
        return jnp.max(s, axis=0, keepdims=True)

    def softmax_chunk(c, cmax, m, l):
        m_new = jnp.maximum(m, cmax)
        alpha = jnp.exp2(m - m_new)
        p = jnp.exp2(s_ref[c % ATT_SCORE_BUFFERS] - m_new)
        l = alpha * l + jnp.sum(p, axis=0, keepdims=True)
        return m_new, l, alpha, p.astype(BF16)

    def accumulate(c, alpha, p):
        pv = jnp.dot(vt_ref[c], p, preferred_element_type=F32)
        acc_ref[...] = acc_ref[...] * alpha + pv

    def start_block(i):
        load_queries(i)
        return tuple(scores(c) for c in range(lookahead))

    def block(i, first_cmax):
        acc_ref[...] = jnp.zeros_like(acc_ref)
        m = jnp.full((1, 2 * tq), MASK_VALUE, F32)
        l = jnp.zeros((1, 2 * tq), F32)
        cmax = dict(enumerate(first_cmax))
        for c in range(n_chunks):
            if c + lookahead < n_chunks:
                cmax[c + lookahead] = scores(c + lookahead)
            m, l, alpha, p = softmax_chunk(c, cmax.pop(c), m, l)
            accumulate(c, alpha, p)
        next_cmax = start_block(jnp.minimum(i + 1, n_blocks - 1))

        o = acc_ref[...] / l
        od = (o[:, :tq] - lam * o[:, tq:]).T
        y = _rmsnorm_rows(od, g_ref[...]) * (1.0 - lambda_init)
        o_ref[pl.ds(pl.multiple_of(i * tq, tq), tq), :] = y.astype(o_ref.dtype)
        return next_cmax

    lax.fori_loop(0, n_blocks, block, start_block(0))


def _diff_attention(proj, lam_init, lq1, lk1, lq2, lk2, subln_g):
    s = proj.shape[0]
    tq = min(ATT_TQ, s)
    tk = min(ATT_TK, s)
    n_chunks = s // tk
    assert n_chunks % ATT_SCORE_BUFFERS == 0, "score slots must line up across query blocks"
    assert ATT_SCORE_BUFFERS >= 2 * ATT_LOOKAHEAD
    kernel_fn = functools.partial(_diff_attn_kernel, tq=tq, tk=tk, n_chunks=n_chunks,
                                  n_blocks=s // tq)
    k_col0 = DIFF_W // LANES
    v_col0 = 2 * DIFF_W // LANES
    vec = lambda n: pl.BlockSpec((1, n), lambda h: (0, 0))
    return pl.pallas_call(
        kernel_fn,
        grid=(N_DIFF_HEADS,),
        in_specs=[pl.BlockSpec(memory_space=pltpu.SMEM),
                  pl.BlockSpec((s, LANES), lambda h: (0, h)),
                  pl.BlockSpec((s, LANES), lambda h: (0, k_col0 + h)),
                  pl.BlockSpec((s, LANES), lambda h: (0, v_col0 + h)),
                  vec(DIFF_QK_DIM), vec(DIFF_QK_DIM), vec(DIFF_QK_DIM), vec(DIFF_QK_DIM),
                  vec(DIFF_V_DIM)],
        out_specs=pl.BlockSpec((s, LANES), lambda h: (0, h)),
        out_shape=jax.ShapeDtypeStruct((s, DIFF_W), BF16),
        scratch_shapes=[pltpu.VMEM((n_chunks, DIFF_V_DIM, tk), BF16),
                        pltpu.VMEM((LANES, 2 * tq), BF16),
                        pltpu.VMEM((ATT_SCORE_BUFFERS, tk, 2 * tq), F32),
                        pltpu.VMEM((DIFF_V_DIM, 2 * tq), F32)],
        compiler_params=_compiler_params(("arbitrary",)),
        name="diff_attention",
    )(lam_init, proj, proj, proj,
      lq1.reshape(1, -1), lk1.reshape(1, -1), lq2.reshape(1, -1), lk2.reshape(1, -1),
      subln_g.reshape(1, -1))


NA_GROUP = 4
NA_SPAN = 3 * NA_GROUP
NA_LOOKAHEAD = 2


def _na_span_start(g, n_groups):
    return jnp.clip(g - 1, 0, n_groups - 3)


def _na_attn_kernel(plan_ref, q_ref, k0_ref, k1_ref, k2_ref, v0_ref, v1_ref, v2_ref, bias_ref,
                    w_f32_ref, o_ref, w_bf16_ref, kwin_ref, vwin_ref, *, n_groups):
    w_bf16_ref[...] = w_f32_ref[...].astype(BF16)
    blk = NA_GROUP * GRID_W
    for t, (k_ref, v_ref) in enumerate(((k0_ref, v0_ref), (k1_ref, v1_ref), (k2_ref, v2_ref))):
        kwin_ref[t * blk:(t + 1) * blk, :] = k_ref[...]
        vwin_ref[t * blk:(t + 1) * blk, :] = v_ref[...]

    g = pl.program_id(0)
    variant = jnp.where(g == 0, 0, jnp.where(g == n_groups - 1, 2, 1))
    pairs = NA_SPAN // 2
    lane_half = lax.broadcasted_iota(jnp.int32, (1, 2 * GRID_W), 1) // GRID_W
    starts, row_masks = [], []
    for j in range(NA_GROUP):
        base = (variant * NA_GROUP + j) * 3
        starts.append(plan_ref[base])
        lo, hi = plan_ref[base + 1], plan_ref[base + 2]
        u = [2 * t + lane_half for t in range(pairs)]
        row_masks.append([(ut >= lo) & (ut < hi) for ut in u])

    def bias_rows(h, j):
        tiles = [jnp.where(row_masks[j][t], bias_ref[h, starts[j] + 2 * t], MASK_VALUE)
                 for t in range(pairs)]
        return jnp.concatenate(tiles, axis=1)

    def head_cols(h):
        return slice(h * NA_HEAD_DIM, (h + 1) * NA_HEAD_DIM)

    def scores(h):
        sc = lax.dot_general(q_ref[:, head_cols(h)], kwin_ref[:, head_cols(h)],
                             (((1,), (1,)), ((), ())),
                             preferred_element_type=F32)
        return sc + jnp.concatenate([bias_rows(h, j) for j in range(NA_GROUP)], axis=0)

    pending = {h: scores(h) for h in range(NA_LOOKAHEAD)}
    for h in range(N_NA_HEADS):
        if h + NA_LOOKAHEAD < N_NA_HEADS:
            pending[h + NA_LOOKAHEAD] = scores(h + NA_LOOKAHEAD)
        sc = pending.pop(h)
        p = jnp.exp2(sc - jnp.max(sc, axis=-1, keepdims=True))
        l = jnp.sum(p, axis=-1, keepdims=True)
        pv = jnp.dot(p.astype(BF16), vwin_ref[:, head_cols(h)], preferred_element_type=F32)
        o_ref[:, head_cols(h)] = pv / l


def _na_window_plan(rows):
    n_groups = rows // NA_GROUP
    kr = NA_WIN_ROWS

    def plan(g):
        start = NA_GROUP * int(np.clip(g - 1, 0, n_groups - 3))
        out = np.full((NA_GROUP, NA_SPAN), -1, np.int64)
        for j in range(NA_GROUP):
            r = NA_GROUP * g + j
            rs = int(np.clip(r - kr // 2, 0, rows - kr))
            for u in range(NA_SPAN):
                if rs <= start + u < rs + kr:
                    out[j, u] = start + u - r + NA_WIN_ROWS - 1
        assert (out >= 0).sum() == NA_GROUP * kr
        return out

    plans = [plan(0), plan(1), plan(n_groups - 1)]
    for g in range(1, n_groups - 1):
        assert (plan(g) == plans[1]).all()
    return plans


def _na_row_windows(rows):
    out = []
    for plan in _na_window_plan(rows):
        for j in range(NA_GROUP):
            valid_u = np.nonzero(plan[j] >= 0)[0]
            assert (np.diff(valid_u) == 1).all() and (np.diff(plan[j, valid_u]) == 1).all()
            out.append((int(plan[j, valid_u[0]] - valid_u[0]), int(valid_u[0]),
                        int(valid_u[-1]) + 1))
    return out


def _na_bias_table(rpb, rows):
    c = np.arange(GRID_W)
    cs = np.clip(c - NA_WIN_COLS // 2, 0, GRID_W - NA_WIN_COLS)
    kc = np.arange(GRID_W)
    col_valid = (kc[None, :] >= cs[:, None]) & (kc[None, :] < cs[:, None] + NA_WIN_COLS)
    pad = GRID_W - NA_WIN_COLS
    rp = jnp.pad(rpb.astype(F32) * LOG2_E, ((0, 0), (0, 0), (pad, pad)))
    by_col = jnp.stack([lax.slice_in_dim(rp, NA_WIN_COLS - 1 - ci + pad,
                                         NA_WIN_COLS - 1 - ci + pad + GRID_W, axis=2)
                        for ci in range(GRID_W)], axis=2)
    by_col = jnp.where(col_valid[None, None], by_col, MASK_VALUE)
    windows = _na_row_windows(rows)
    front = max(0, -min(w[0] for w in windows))
    back = max(0, max(w[0] for w in windows) + NA_SPAN - by_col.shape[1])
    padded = jnp.pad(by_col, ((0, 0), (front, back + 1), (0, 0), (0, 0)))
    table = jnp.concatenate([padded[:, :-1], padded[:, 1:]], axis=-1)
    plan = np.array([[w[0] + front, w[1], w[2]] for w in windows], np.int32).reshape(-1)
    return table, jnp.asarray(plan)


def _na_attention(proj, bias, plan, w_out, layer):
    s = proj.shape[0]
    rows = s // GRID_W
    assert rows % NA_GROUP == 0 and rows >= NA_SPAN and NA_SPAN >= NA_GROUP + NA_WIN_ROWS - 1
    assert NA_SPAN % 2 == 0
    n_groups = rows // NA_GROUP
    blk = NA_GROUP * GRID_W
    q_col = 3 * DIFF_W // NA_W
    k_col, v_col = q_col + 1, q_col + 2
    bias = bias.reshape((-1, N_NA_HEADS) + bias.shape[1:])
    w_rows, w_cols = w_out.shape[1:]
    slab = w_rows // n_groups
    assert slab * n_groups == w_rows and slab % 16 == 0

    def kv_spec(t, col):
        return pl.BlockSpec((blk, NA_W),
                            lambda g, plan_ref: (_na_span_start(g, n_groups) + t, col))

    grid_spec = pltpu.PrefetchScalarGridSpec(
        num_scalar_prefetch=1,
        grid=(n_groups,),
        in_specs=[pl.BlockSpec((blk, NA_W), lambda g, plan_ref: (g, q_col))]
                 + [kv_spec(t, k_col) for t in range(3)]
                 + [kv_spec(t, v_col) for t in range(3)]
                 + [pl.BlockSpec((None,) + bias.shape[1:],
                                 lambda g, plan_ref: (layer, 0, 0, 0, 0)),
                    pl.BlockSpec((None, slab, w_cols), lambda g, plan_ref: (layer, g, 0))],
        out_specs=(pl.BlockSpec((blk, NA_W), lambda g, plan_ref: (g, 0)),
                   pl.BlockSpec((slab, w_cols), lambda g, plan_ref: (g, 0))),
        scratch_shapes=[pltpu.VMEM((NA_SPAN * GRID_W, NA_W), BF16),
                        pltpu.VMEM((NA_SPAN * GRID_W, NA_W), BF16)])
    return pl.pallas_call(
        functools.partial(_na_attn_kernel, n_groups=n_groups),
        grid_spec=grid_spec,
        out_shape=(jax.ShapeDtypeStruct((s, NA_W), F32),
                   jax.ShapeDtypeStruct((w_rows, w_cols), BF16)),
        compiler_params=_compiler_params(("arbitrary",)),
        name="na_attention",
    )(plan, *([proj] * 7), bias, w_out)


def _out_proj_kernel(a_ref, b_ref, na_g_ref, w_ref, x_ref, mlp_g_ref, o_ref, h_ref):
    ka = a_ref.shape[1]
    tr = a_ref.shape[0] // OUT_PROJ_ROW_SPLIT
    row_slices = [slice(r * tr, (r + 1) * tr) for r in range(OUT_PROJ_ROW_SPLIT)]

    def normed_b(r):
        return _rmsnorm_rows(b_ref[row_slices[r], :], na_g_ref[...]).astype(BF16)

    b = normed_b(0)
    for r, rows in enumerate(row_slices):
        b_next = normed_b(r + 1) if r + 1 < OUT_PROJ_ROW_SPLIT else None
        acc = jnp.dot(a_ref[rows, :], w_ref[:ka, :], preferred_element_type=F32)
        acc += jnp.dot(b, w_ref[ka:, :], preferred_element_type=F32)
        x_new = x_ref[rows, :] + acc
        o_ref[rows, :] = x_new
        h_ref[rows, :] = _rmsnorm_rows(x_new, mlp_g_ref[...]).astype(h_ref.dtype)
        b = b_next


def _out_proj(o_diff, o_na, na_g, w, x, mlp_g):
    s, d = x.shape
    tm = min(OUT_PROJ_TM, s)
    ka, kb = o_diff.shape[1], o_na.shape[1]
    row_block = lambda width: pl.BlockSpec((tm, width), lambda i: (i, 0))
    vec = lambda width: pl.BlockSpec((1, width), lambda i: (0, 0))
    return pl.pallas_call(
        _out_proj_kernel,
        grid=(s // tm,),
        in_specs=[row_block(ka), row_block(kb), vec(kb),
                  pl.BlockSpec((ka + kb, d), lambda i: (0, 0)),
                  row_block(d), vec(d)],
        out_specs=(row_block(d), row_block(d)),
        out_shape=(jax.ShapeDtypeStruct((s, d), F32), jax.ShapeDtypeStruct((s, d), BF16)),
        compiler_params=_compiler_params(("arbitrary",)),
        name="out_proj",
    )(o_diff, o_na, na_g.reshape(1, kb), w, x, mlp_g.reshape(1, d))


def _mlp_out_kernel(u_ref, w_ref, x_ref, g_ref, o_ref, *maybe_h_ref):
    k = pl.program_id(1)

    @pl.when(k == 0)
    def _():
        o_ref[...] = x_ref[...]

    is_last = k == pl.num_programs(1) - 1

    @pl.when(jnp.logical_not(is_last))
    def _():
        o_ref[...] += jnp.dot(u_ref[...], w_ref[...], preferred_element_type=F32)

    @pl.when(is_last)
    def _():
        tr = o_ref.shape[0] // MM_ROW_SPLIT
        for r in range(MM_ROW_SPLIT):
            rows = slice(r * tr, (r + 1) * tr)
            x_new = o_ref[rows, :] + jnp.dot(u_ref[rows, :], w_ref[...],
                                             preferred_element_type=F32)
            normed = _rmsnorm_rows(x_new, g_ref[...])
            if maybe_h_ref:
                o_ref[rows, :] = x_new
                maybe_h_ref[0][rows, :] = normed.astype(maybe_h_ref[0].dtype)
            else:
                o_ref[rows, :] = normed


def _mlp_out(u, w, x, g, is_last_layer):
    s, d = x.shape
    f = u.shape[1]
    tm = min(MM_TM, s)
    row_block = pl.BlockSpec((tm, d), lambda i, k: (i, 0))
    x_shape = jax.ShapeDtypeStruct((s, d), F32)
    return pl.pallas_call(
        _mlp_out_kernel,
        grid=(s // tm, f // MLP_OUT_TK),
        in_specs=[pl.BlockSpec((tm, MLP_OUT_TK), lambda i, k: (i, k)),
                  pl.BlockSpec((MLP_OUT_TK, d), lambda i, k: (k, 0)),
                  row_block, pl.BlockSpec((1, d), lambda i, k: (0, 0))],
        out_specs=row_block if is_last_layer else (row_block, row_block),
        out_shape=x_shape if is_last_layer else (x_shape, jax.ShapeDtypeStruct((s, d), BF16)),
        compiler_params=_compiler_params(("arbitrary", "arbitrary")),
        name="mlp_out_final" if is_last_layer else "mlp_out",
    )(u, w, x, g.reshape(1, d))


def _rope_tables(seq):
    inv_freq = 1.0 / (ROPE_THETA ** (jnp.arange(0, DIFF_QK_DIM, 2, dtype=F32) / DIFF_QK_DIM))
    ang = jnp.arange(seq, dtype=F32)[:, None] * inv_freq[None, :]
    cos, sin = jnp.cos(ang), jnp.sin(ang)
    zero = jnp.zeros_like(sin)
    reps = LANES // DIFF_QK_DIM
    cos_t = jnp.tile(cos, (1, 2 * reps))
    sin_lo = jnp.tile(jnp.concatenate([-sin, zero], axis=1), (1, reps))
    sin_hi = jnp.tile(jnp.concatenate([zero, sin], axis=1), (1, reps))
    return cos_t, sin_lo, sin_hi


def kernel(x, attn_norm, w_in, lambda_q1, lambda_k1, lambda_q2, lambda_k2, diff_subln, na_norm,
           na_rpb, w_out, mlp_norm, w_mlp_in, w_mlp_out, final_norm):
    b, s, d = x.shape
    depth = w_in.shape[0]
    rope = _rope_tables(s)
    na_bias, na_plan = _na_bias_table(
        na_rpb.reshape((depth * N_NA_HEADS,) + na_rpb.shape[2:]), s // GRID_W)
    w_in, w_out, w_mlp_in, w_mlp_out = (w.astype(F32) for w in (w_in, w_out, w_mlp_in, w_mlp_out))
    outs = []
    for bi in range(b):
        xb = x[bi].astype(F32)
        h_attn = _rmsnorm_cast(xb, attn_norm[0].astype(F32))
        for l in range(depth):
            lambda_init = 0.8 - 0.6 * math.exp(-0.3 * l)
            proj = _projection(_in_proj_kernel, h_attn, w_in, l, IN_PROJ_TM, MM_TN, rope,
                               name="in_proj")
            o_diff = _diff_attention(proj, jnp.full((1,), lambda_init, F32),
                                     lambda_q1[l].astype(F32), lambda_k1[l].astype(F32),
                                     lambda_q2[l].astype(F32), lambda_k2[l].astype(F32),
                                     diff_subln[l].astype(F32))
            o_na, w_out_l = _na_attention(proj, na_bias, na_plan, w_out, l)
            xb, h_mlp = _out_proj(o_diff, o_na, na_norm[l].astype(F32), w_out_l, xb,
                                  mlp_norm[l].astype(F32))
            u, w_mlp_out_l = _mlp_in(h_mlp, w_mlp_in, w_mlp_out, l)
            if l == depth - 1:
                xb = _mlp_out(u, w_mlp_out_l, xb, final_norm.astype(F32), True)
            else:
                xb, h_attn = _mlp_out(u, w_mlp_out_l, xb, attn_norm[l + 1].astype(F32), False)
        outs.append(xb)
    out = outs[0][None] if b == 1 else jnp.stack(outs)
    return out.astype(x.dtype)
```
